```python
import math
import jax, jax.numpy as jnp
from jax import lax
import numpy as np

D_MODEL = 2048
BATCH = 8
SEQ = 4096
DEPTH = 4
DEC_BATCH = 4
DEC_SEQ = 8192
PAST_LEN = 128

PLE_DIM = 256
GRID_W = 64
N_MIXERS = 3
Q_BLOCK = 128
EPS = 1e-6
D_FF = 4 * D_MODEL

A_HEADS = 16
A_KV_HEADS = 4
A_HEAD_DIM = D_MODEL // A_HEADS
A_GROUP = A_HEADS // A_KV_HEADS
A_ROPE_THETA = 10000.0

B_HEADS = 16
B_Q_RANK = 512
B_KV_RANK = 512
B_NOPE_DIM = 128
B_ROPE_DIM = 64
B_V_DIM = D_MODEL // B_HEADS
B_QK_DIM = B_NOPE_DIM + B_ROPE_DIM
B_ROPE_THETA = 10000.0

C_PAIRS = ((128, 1), (512, 4), (2048, 16))
C_GROUPS = len(C_PAIRS)
C_HEADS = 8
C_HEAD_DIM = 128
C_ROPE_DIM = C_HEAD_DIM // 4
C_ROPE_THETA = 500000.0

N_A = len(range(0, DEPTH, N_MIXERS))
N_B = len(range(1, DEPTH, N_MIXERS))
N_C = len(range(2, DEPTH, N_MIXERS))

kernel_name = 'hybrid_gqa_mla_dilated_encoder'


def rmsnorm(x, g):
    xf = x.astype(jnp.float32)
    y = xf * lax.rsqrt(jnp.mean(xf * xf, axis=-1, keepdims=True) + EPS)
    return (y * g.astype(jnp.float32)).astype(x.dtype)


def rope_tables(pos, dim, theta):
    inv_freq = theta ** (-jnp.arange(0, dim, 2, dtype=jnp.float32) / dim)
    ang = pos.astype(jnp.float32)[:, None] * inv_freq[None, :]
    return jnp.cos(ang), jnp.sin(ang)


def apply_rope(x, cos, sin):
    shape = (cos.shape[0],) + (1,) * (x.ndim - 3) + (cos.shape[1],)
    c = cos.reshape(shape)
    s = sin.reshape(shape)
    xf = x.astype(jnp.float32)
    x1, x2 = jnp.split(xf, 2, axis=-1)
    return jnp.concatenate([x1 * c - x2 * s, x2 * c + x1 * s], axis=-1).astype(x.dtype)


def dense_attention_blocked(q, k, v, scale):
    b, s = q.shape[:2]
    nb = s // Q_BLOCK
    qb = jnp.swapaxes(q.reshape((b, nb, Q_BLOCK) + q.shape[2:]), 0, 1)

    def one_block(q_blk):
        sc = jnp.einsum('bqkgd,bskd->bkgqs', q_blk, k,
                        preferred_element_type=jnp.float32) * scale
        pr = jax.nn.softmax(sc, axis=-1)
        return jnp.einsum('bkgqs,bskd->bqkgd', pr.astype(v.dtype), v)

    o = lax.map(one_block, qb)
    return jnp.swapaxes(o, 0, 1).reshape((b, s) + o.shape[3:])


def dilated_window_attention(q, k, v, window, dilation):
    b, s, h, dh = q.shape
    d = dilation
    L = s // d
    R = window // (2 * d)
    qb_len = math.gcd(L, Q_BLOCK)
    nb = L // qb_len
    kw = qb_len + 2 * R

    def to_sub(a):
        rest = a.shape[2:]
        a = a.reshape((b, L, d) + rest)
        return jnp.swapaxes(a, 1, 2).reshape((b * d, L) + rest)

    def from_sub(a):
        rest = a.shape[2:]
        a = a.reshape((b, d, L) + rest)
        return jnp.swapaxes(a, 1, 2).reshape((b, s) + rest)

    qs, ks, vs = to_sub(q), to_sub(k), to_sub(v)
    pad = ((0, 0), (R, R), (0, 0), (0, 0))
    kp = jnp.pad(ks, pad)
    vp = jnp.pad(vs, pad)
    key_idx = np.arange(nb)[:, None] * qb_len + np.arange(kw)[None, :]
    kb = kp[:, key_idx]
    vb = vp[:, key_idx]
    qb = qs.reshape((b * d, nb, qb_len, h, dh))

    r_loc = np.arange(qb_len)[:, None]
    c_loc = np.arange(kw)[None, :]
    band = np.abs(c_loc - R - r_loc) <= R
    key_pos = np.arange(nb)[:, None, None] * qb_len + c_loc[None] - R
    valid = jnp.asarray(band[None] & (key_pos >= 0) & (key_pos < L))

    sc = jnp.einsum('nbqhd,nbkhd->nbhqk', qb, kb,
                    preferred_element_type=jnp.float32) * (dh ** -0.5)
    sc = jnp.where(valid[None, :, None], sc, -jnp.inf)
    m = jnp.max(sc, axis=-1, keepdims=True)
    e = jnp.exp(sc - m)
    den = jnp.sum(e, axis=-1, keepdims=True)
    o = jnp.einsum('nbhqk,nbkhd->nbqhd', (e / den).astype(v.dtype), vb)
    lse = jnp.swapaxes((m + jnp.log(den))[..., 0], 2, 3)
    o = o.reshape((b * d, L, h, dh))
    lse = lse.reshape((b * d, L, h))
    return from_sub(o), from_sub(lse)


def mixer_a(h, w_qkv, q_g, k_g, w_o):
    b, s, _ = h.shape
    rows = s // GRID_W
    row = jnp.repeat(jnp.arange(rows), GRID_W)
    col = jnp.tile(jnp.arange(GRID_W), rows)
    half = A_HEAD_DIM // 2
    cr, sr = rope_tables(row, half, A_ROPE_THETA)
    cc, scol = rope_tables(col, half, A_ROPE_THETA)

    def axial(t):
        return jnp.concatenate([apply_rope(t[..., :half], cr, sr),
                                apply_rope(t[..., half:], cc, scol)], axis=-1)

    qkv = h @ w_qkv
    q, k, v = jnp.split(qkv, [A_HEADS * A_HEAD_DIM, (A_HEADS + A_KV_HEADS) * A_HEAD_DIM], axis=-1)
    q = axial(rmsnorm(q.reshape(b, s, A_HEADS, A_HEAD_DIM), q_g))
    k = axial(rmsnorm(k.reshape(b, s, A_KV_HEADS, A_HEAD_DIM), k_g))
    v = v.reshape(b, s, A_KV_HEADS, A_HEAD_DIM)
    o = dense_attention_blocked(q.reshape(b, s, A_KV_HEADS, A_GROUP, A_HEAD_DIM), k, v,
                                A_HEAD_DIM ** -0.5)
    return o.reshape(b, s, A_HEADS * A_HEAD_DIM) @ w_o


def mixer_b(h, w_dqkv, cq_g, ckv_g, w_uq, w_ukv, q_g, k_g, w_o):
    b, s, _ = h.shape
    lat = h @ w_dqkv
    cq, ckv, k_rope = jnp.split(lat, [B_Q_RANK, B_Q_RANK + B_KV_RANK], axis=-1)
    q = (rmsnorm(cq, cq_g) @ w_uq).reshape(b, s, B_HEADS, B_QK_DIM)
    kv = (rmsnorm(ckv, ckv_g) @ w_ukv).reshape(b, s, B_HEADS, B_NOPE_DIM + B_V_DIM)
    k_nope, v = jnp.split(kv, [B_NOPE_DIM], axis=-1)
    k = jnp.concatenate(
        [k_nope, jnp.broadcast_to(k_rope[:, :, None, :], (b, s, B_HEADS, B_ROPE_DIM))], axis=-1)
    q = rmsnorm(q, q_g)
    k = rmsnorm(k, k_g)
    cos, sin = rope_tables(jnp.arange(s), B_ROPE_DIM, B_ROPE_THETA)
    q = jnp.concatenate([q[..., :B_NOPE_DIM], apply_rope(q[..., B_NOPE_DIM:], cos, sin)], axis=-1)
    k = jnp.concatenate([k[..., :B_NOPE_DIM], apply_rope(k[..., B_NOPE_DIM:], cos, sin)], axis=-1)
    o = dense_attention_blocked(q[:, :, :, None, :], k, v, B_QK_DIM ** -0.5)
    return o.reshape(b, s, B_HEADS * B_V_DIM) @ w_o


def mixer_c(h, w_qkv, q_g, k_g, w_o):
    b, s, _ = h.shape
    qkv = (h @ w_qkv).reshape(b, s, 3, C_GROUPS, C_HEADS, C_HEAD_DIM)
    q = rmsnorm(qkv[:, :, 0], q_g)
    k = rmsnorm(qkv[:, :, 1], k_g)
    v = qkv[:, :, 2]
    cos, sin = rope_tables(jnp.arange(s), C_ROPE_DIM, C_ROPE_THETA)
    q = jnp.concatenate([apply_rope(q[..., :C_ROPE_DIM], cos, sin), q[..., C_ROPE_DIM:]], axis=-1)
    k = jnp.concatenate([apply_rope(k[..., :C_ROPE_DIM], cos, sin), k[..., C_ROPE_DIM:]], axis=-1)
    outs, lses = [], []
    for g, (window, dilation) in enumerate(C_PAIRS):
        o_g, lse_g = dilated_window_attention(q[:, :, g], k[:, :, g], v[:, :, g], window, dilation)
        outs.append(o_g)
        lses.append(lse_g)
    o = jnp.stack(outs, axis=0)
    wts = jax.nn.softmax(jnp.stack(lses, axis=0), axis=0)
    merged = jnp.sum(wts[..., None] * o.astype(jnp.float32), axis=0).astype(h.dtype)
    return merged.reshape(b, s, C_HEADS * C_HEAD_DIM) @ w_o


def squared_relu_mlp(h, w_in, w_out):
    return jnp.square(jax.nn.relu(h @ w_in)) @ w_out


def layer_stack(x, p, weights):
    (norm_mix_g, norm_mlp_g, w_mlp_in, w_mlp_out, w_ple, ple_norm_g, ple_gate_norm_g, w_ple_gate,
     a_w_qkv, a_q_norm_g, a_k_norm_g, a_w_o,
     b_w_dqkv, b_cq_norm_g, b_ckv_norm_g, b_w_uq, b_w_ukv, b_q_norm_g, b_k_norm_g, b_w_o,
     c_w_qkv, c_q_norm_g, c_k_norm_g, c_w_o) = weights
    for i in range(DEPTH):
        kind, j = i % N_MIXERS, i // N_MIXERS
        h = rmsnorm(x, norm_mix_g[i])
        if kind == 0:
            y = mixer_a(h, a_w_qkv[j], a_q_norm_g[j], a_k_norm_g[j], a_w_o[j])
        elif kind == 1:
            y = mixer_b(h, b_w_dqkv[j], b_cq_norm_g[j], b_ckv_norm_g[j], b_w_uq[j], b_w_ukv[j],
                        b_q_norm_g[j], b_k_norm_g[j], b_w_o[j])
        else:
            y = mixer_c(h, c_w_qkv[j], c_q_norm_g[j], c_k_norm_g[j], c_w_o[j])
        x = x + y
        x = x + squared_relu_mlp(rmsnorm(x, norm_mlp_g[i]), w_mlp_in[i], w_mlp_out[i])
        e = rmsnorm(p[i] @ w_ple[i], ple_norm_g[i])
        gate = jax.nn.sigmoid(rmsnorm(x, ple_gate_norm_g[i]) @ w_ple_gate[i])
        x = x + gate * e
    return x


def setup_inputs(seed: int = 0) -> dict:
    key = jax.random.key(seed)
    ks = iter([jax.random.fold_in(key, i) for i in range(32)])

    def normal(shape, scale):
        return jax.random.normal(next(ks), shape, jnp.float32) * scale

    def gain(shape):
        return 1.0 + normal(shape, 0.02)

    res = (2.0 * DEPTH) ** -0.5
    a_qkv_w = (A_HEADS + 2 * A_KV_HEADS) * A_HEAD_DIM
    return {
        'x_prompt': normal((BATCH, SEQ, D_MODEL), 1.0),
        'x_sample': normal((DEC_BATCH, DEC_SEQ, D_MODEL), 1.0),
        'p_prompt': normal((DEPTH, BATCH, SEQ, PLE_DIM), 1.0),
        'p_sample': normal((DEPTH, DEC_BATCH, DEC_SEQ, PLE_DIM), 1.0),
        'norm_mix_g': gain((DEPTH, D_MODEL)),
        'norm_mlp_g': gain((DEPTH, D_MODEL)),
        'w_mlp_in': normal((DEPTH, D_MODEL, D_FF), D_MODEL ** -0.5),
        'w_mlp_out': normal((DEPTH, D_FF, D_MODEL), res * D_FF ** -0.5),
        'w_ple': normal((DEPTH, PLE_DIM, D_MODEL), PLE_DIM ** -0.5),
        'ple_norm_g': gain((DEPTH, D_MODEL)),
        'ple_gate_norm_g': gain((DEPTH, D_MODEL)),
        'w_ple_gate': normal((DEPTH, D_MODEL, D_MODEL), D_MODEL ** -0.5),
        'a_w_qkv': normal((N_A, D_MODEL, a_qkv_w), D_MODEL ** -0.5),
        'a_q_norm_g': gain((N_A, A_HEAD_DIM)),
        'a_k_norm_g': gain((N_A, A_HEAD_DIM)),
        'a_w_o': normal((N_A, A_HEADS * A_HEAD_DIM, D_MODEL), res * (A_HEADS * A_HEAD_DIM) ** -0.5),
        'b_w_dqkv': normal((N_B, D_MODEL, B_Q_RANK + B_KV_RANK + B_ROPE_DIM), D_MODEL ** -0.5),
        'b_cq_norm_g': gain((N_B, B_Q_RANK)),
        'b_ckv_norm_g': gain((N_B, B_KV_RANK)),
        'b_w_uq': normal((N_B, B_Q_RANK, B_HEADS * B_QK_DIM), B_Q_RANK ** -0.5),
        'b_w_ukv': normal((N_B, B_KV_RANK, B_HEADS * (B_NOPE_DIM + B_V_DIM)), B_KV_RANK ** -0.5),
        'b_q_norm_g': gain((N_B, B_QK_DIM)),
        'b_k_norm_g': gain((N_B, B_QK_DIM)),
        'b_w_o': normal((N_B, B_HEADS * B_V_DIM, D_MODEL), res * (B_HEADS * B_V_DIM) ** -0.5),
        'c_w_qkv': normal((N_C, D_MODEL, 3 * C_GROUPS * C_HEADS * C_HEAD_DIM), D_MODEL ** -0.5),
        'c_q_norm_g': gain((N_C, C_HEAD_DIM)),
        'c_k_norm_g': gain((N_C, C_HEAD_DIM)),
        'c_w_o': normal((N_C, C_HEADS * C_HEAD_DIM, D_MODEL), res * (C_HEADS * C_HEAD_DIM) ** -0.5),
    }


def reference(x_prompt, x_sample, p_prompt, p_sample,
              norm_mix_g, norm_mlp_g, w_mlp_in, w_mlp_out, w_ple, ple_norm_g, ple_gate_norm_g,
              w_ple_gate,
              a_w_qkv, a_q_norm_g, a_k_norm_g, a_w_o,
              b_w_dqkv, b_cq_norm_g, b_ckv_norm_g, b_w_uq, b_w_ukv, b_q_norm_g, b_k_norm_g, b_w_o,
              c_w_qkv, c_q_norm_g, c_k_norm_g, c_w_o):
    weights = (norm_mix_g, norm_mlp_g, w_mlp_in, w_mlp_out, w_ple, ple_norm_g, ple_gate_norm_g,
               w_ple_gate,
               a_w_qkv, a_q_norm_g, a_k_norm_g, a_w_o,
               b_w_dqkv, b_cq_norm_g, b_ckv_norm_g, b_w_uq, b_w_ukv, b_q_norm_g, b_k_norm_g, b_w_o,
               c_w_qkv, c_q_norm_g, c_k_norm_g, c_w_o)
    y_prompt = layer_stack(x_prompt, p_prompt, weights)
    y_sample = layer_stack(x_sample, p_sample, weights)
    return (y_prompt, y_sample)
```

```python
import functools
import math

import numpy as np
import jax
import jax.numpy as jnp
from jax import lax
from jax.experimental import pallas as pl
from jax.experimental.pallas import tpu as pltpu

F32 = jnp.float32
BF16 = jnp.bfloat16

D_MODEL = 2048
DEPTH = 4
N_MIXERS = 3
PLE_DIM = 256
GRID_W = 64
EPS = 1e-6
D_FF = 4 * D_MODEL

A_HEADS = 16
A_KV_HEADS = 4
A_HEAD_DIM = 128
A_ROPE_THETA = 10000.0

B_HEADS = 16
B_Q_RANK = 512
B_KV_RANK = 512
B_NOPE_DIM = 128
B_ROPE_DIM = 64
B_V_DIM = 128
B_QK_DIM = B_NOPE_DIM + B_ROPE_DIM
B_ROPE_THETA = 10000.0
B_HEAD_PAD = 256
B_LAT_PAD = 1152

C_PAIRS = ((128, 1), (512, 4), (2048, 16))
C_GROUPS = 3
C_HEADS = 8
C_HEAD_DIM = 128
C_ROPE_DIM = 32
C_ROPE_THETA = 500000.0

LANES = 128
LOG2E = math.log2(math.e)
MASK_BIAS = -1e30
VMEM_LIMIT = 56 * 1024 * 1024


def _params(n_axes):
    return pltpu.CompilerParams(dimension_semantics=("arbitrary",) * n_axes,
                                vmem_limit_bytes=VMEM_LIMIT)


def _rms_to_bf16(x_ref, g_ref, xn_ref, rows=128):
    tm = x_ref.shape[0]

    def chunk(c, carry):
        r0 = pl.multiple_of(c * rows, rows)
        xf = x_ref[pl.ds(r0, rows), :]
        ms = jnp.mean(xf * xf, axis=-1, keepdims=True)
        xn_ref[pl.ds(r0, rows), :] = (xf * lax.rsqrt(ms + EPS) * g_ref[...]).astype(BF16)
        return carry

    lax.fori_loop(0, tm // rows, chunk, 0)


def _rope(y, cos, sin, half):
    lane = lax.broadcasted_iota(jnp.int32, y.shape, 1)
    low = (lane & (2 * half - 1)) < half
    partner = jnp.where(low, pltpu.roll(y, LANES - half, 1), pltpu.roll(y, half, 1))
    return y * cos + partner * sin


def _fused_matmul(x, w, *, tm, tn, gain=None, x_kblock=0, extras=(), epilogue=None,
                  out_tile=None, out_dtype=BF16):
    t_rows = x.shape[0]
    k_dim, n_dim = w.shape
    nj = n_dim // tn
    ow = out_tile or tn
    norm = gain is not None
    n_extra = len(extras)

    def body(*refs):
        x_ref, w_ref = refs[0], refs[1]
        pos = 2
        if norm:
            g_ref = refs[pos]
            pos += 1
        extra_refs = refs[pos:pos + n_extra]
        o_ref = refs[pos + n_extra]
        j = pl.program_id(1)
        if norm:
            xn_ref = refs[pos + n_extra + 1]

            @pl.when(j == 0)
            def _():
                _rms_to_bf16(x_ref, g_ref, xn_ref)

            lhs = xn_ref[...]
        else:
            lhs = x_ref[...]
        acc = jnp.dot(lhs, w_ref[...], preferred_element_type=F32)
        if epilogue is None:
            o_ref[...] = acc.astype(o_ref.dtype)
        else:
            epilogue(acc, j, extra_refs, o_ref)

    in_specs = [pl.BlockSpec((tm, k_dim), lambda i, j: (i, x_kblock)),
                pl.BlockSpec((k_dim, tn), lambda i, j: (0, j))]
    args = [x, w]
    if norm:
        in_specs.append(pl.BlockSpec((1, k_dim), lambda i, j: (0, 0)))
        args.append(gain)
    for arr, spec in extras:
        in_specs.append(spec)
        args.append(arr)
    scratch = [pltpu.VMEM((tm, k_dim), BF16)] if norm else []
    return pl.pallas_call(
        body,
        grid=(t_rows // tm, nj),
        in_specs=in_specs,
        out_specs=pl.BlockSpec((tm, ow), lambda i, j: (i, j)),
        out_shape=jax.ShapeDtypeStruct((t_rows, nj * ow), out_dtype),
        scratch_shapes=scratch,
        compiler_params=_params(2),
    )(*args)


def _residual_epilogue(acc, j, extra_refs, o_ref):
    (res_ref,) = extra_refs
    o_ref[...] = res_ref[...] + acc


def _out_proj_residual(o, w_o, x, *, tm=1024, tn=512):
    extras = [(x, pl.BlockSpec((tm, tn), lambda i, j: (i, j)))]
    return _fused_matmul(o, w_o, tm=tm, tn=tn, extras=extras,
                         epilogue=_residual_epilogue, out_dtype=F32)


def _head_norm_rope_epilogue(n_normed_tiles, half, tn):
    def ep(acc, j, extra_refs, o_ref):
        g_ref, cos_ref, sin_ref = extra_refs

        @pl.when(j < n_normed_tiles)
        def _():
            for h in range(tn // LANES):
                sl = slice(h * LANES, (h + 1) * LANES)
                c = acc[:, sl]
                ms = jnp.mean(c * c, axis=-1, keepdims=True)
                y = c * lax.rsqrt(ms + EPS) * g_ref[:, sl]
                o_ref[:, sl] = _rope(y, cos_ref[...], sin_ref[...], half).astype(o_ref.dtype)

        @pl.when(j >= n_normed_tiles)
        def _():
            o_ref[...] = acc.astype(o_ref.dtype)

    return ep


def _qkv_proj(x, w, norm_g, head_g, cos, sin, *, seq, n_normed_tiles, half, tm=1024, tn=512):
    n_pos_blocks = seq // tm
    extras = [
        (head_g, pl.BlockSpec((1, tn), lambda i, j: (0, j))),
        (cos, pl.BlockSpec((tm, LANES), lambda i, j: (i % n_pos_blocks, 0))),
        (sin, pl.BlockSpec((tm, LANES), lambda i, j: (i % n_pos_blocks, 0))),
    ]
    return _fused_matmul(x, w, tm=tm, tn=tn, gain=norm_g, extras=extras,
                         epilogue=_head_norm_rope_epilogue(n_normed_tiles, half, tn))


def _mla_q_epilogue(tn):
    def ep(acc, j, extra_refs, o_ref):
        g_ref, cos_ref, sin_ref = extra_refs
        for h in range(tn // B_HEAD_PAD):
            lo = slice(h * B_HEAD_PAD, h * B_HEAD_PAD + LANES)
            hi = slice(h * B_HEAD_PAD + LANES, (h + 1) * B_HEAD_PAD)
            c_lo, c_hi = acc[:, lo], acc[:, hi]
            ss = (jnp.sum(c_lo * c_lo, axis=-1, keepdims=True)
                  + jnp.sum(c_hi * c_hi, axis=-1, keepdims=True)) * (1.0 / B_QK_DIM)
            r = lax.rsqrt(ss + EPS)
            o_ref[:, lo] = (c_lo * r * g_ref[:, :LANES]).astype(o_ref.dtype)
            y_hi = c_hi * r * g_ref[:, LANES:]
            o_ref[:, hi] = _rope(y_hi, cos_ref[...], sin_ref[...], B_ROPE_DIM // 2).astype(o_ref.dtype)

    return ep


def _mla_k_epilogue(tn):
    def ep(acc, j, extra_refs, o_ref):
        kr_ref, g_ref, cos_ref, sin_ref = extra_refs
        kr = kr_ref[...]
        kr_ss = jnp.sum(kr * kr, axis=-1, keepdims=True)
        for h in range(tn // LANES):
            c = acc[:, h * LANES:(h + 1) * LANES]
            ss = (jnp.sum(c * c, axis=-1, keepdims=True) + kr_ss) * (1.0 / B_QK_DIM)
            r = lax.rsqrt(ss + EPS)
            lo = slice(h * B_HEAD_PAD, h * B_HEAD_PAD + LANES)
            hi = slice(h * B_HEAD_PAD + LANES, (h + 1) * B_HEAD_PAD)
            o_ref[:, lo] = (c * r * g_ref[:, :LANES]).astype(o_ref.dtype)
            y_hi = kr * r * g_ref[:, LANES:]
            o_ref[:, hi] = _rope(y_hi, cos_ref[...], sin_ref[...], B_ROPE_DIM // 2).astype(o_ref.dtype)

    return ep


def _dense_attention(q_arr, k_arr, v_arr, *, batch, seq, n_groups, heads, kv_heads, dq,
                     q_cb0, k_cb0, v_cb0, scale, tq=512, tk=512):
    nq, nk = seq // tq, seq // tk
    c = scale * LOG2E
    dv = LANES

    def body(q_ref, k_ref, v_ref, o_ref, qt_ref, m_ref, l_ref, acc_ref):
        ki = pl.program_id(3)

        @pl.when(ki == 0)
        def _():
            for g in range(heads):
                qg = q_ref[:, g * dq:(g + 1) * dq].astype(F32)
                qt_ref[g] = qg.T.astype(BF16)
            m_ref[...] = jnp.full(m_ref.shape, -jnp.inf, F32)
            l_ref[...] = jnp.zeros(l_ref.shape, F32)
            acc_ref[...] = jnp.zeros(acc_ref.shape, F32)

        for g in range(heads):
            kv = g * kv_heads // heads
            k = k_ref[:, kv * dq:(kv + 1) * dq]
            v = v_ref[:, kv * dv:(kv + 1) * dv]
            s = jnp.dot(k, qt_ref[g], preferred_element_type=F32) * c
            m_prev = m_ref[g]
            m_new = jnp.maximum(m_prev, jnp.max(s, axis=0, keepdims=True))
            alpha = jnp.exp2(m_prev - m_new)
            p = jnp.exp2(s - m_new)
            l_ref[g] = alpha * l_ref[g] + jnp.sum(p, axis=0, keepdims=True)
            pv = lax.dot_general(v, p.astype(BF16), (((0,), (0,)), ((), ())),
                                 preferred_element_type=F32)
            acc_ref[g] = acc_ref[g] * alpha + pv
            m_ref[g] = m_new

        @pl.when(ki == nk - 1)
        def _():
            for g in range(heads):
                o = acc_ref[g] / l_ref[g]
                o_ref[:, g * dv:(g + 1) * dv] = o.T.astype(o_ref.dtype)

    t_rows = batch * seq
    return pl.pallas_call(
        body,
        grid=(batch, n_groups, nq, nk),
        in_specs=[
            pl.BlockSpec((tq, heads * dq), lambda b, g, qi, ki: (b * nq + qi, q_cb0 + g)),
            pl.BlockSpec((tk, kv_heads * dq), lambda b, g, qi, ki: (b * nk + ki, k_cb0 + g)),
            pl.BlockSpec((tk, kv_heads * dv), lambda b, g, qi, ki: (b * nk + ki, v_cb0 + g)),
        ],
        out_specs=pl.BlockSpec((tq, heads * dv), lambda b, g, qi, ki: (b * nq + qi, g)),
        out_shape=jax.ShapeDtypeStruct((t_rows, n_groups * heads * dv), BF16),
        scratch_shapes=[
            pltpu.VMEM((heads, dq, tq), BF16),
            pltpu.VMEM((heads, 1, tq), F32),
            pltpu.VMEM((heads, 1, tq), F32),
            pltpu.VMEM((heads, dv, tq), F32),
        ],
        compiler_params=_params(4),
    )(q_arr, k_arr, v_arr)


C_TQ = 256


def _dilated_steps(tq):
    steps = []
    for g, (window, dil) in enumerate(C_PAIRS):
        reach = -(-(window // 2) // tq)
        offs = [0] + [o for o in range(-reach, reach + 1) if o != 0]
        steps += [(g, o) for o in offs]
    return steps


def _dilated_bias(tq):
    steps = _dilated_steps(tq)
    r = np.arange(tq)[:, None]
    col = np.arange(tq)[None, :]
    out = np.empty((len(steps), tq, tq), np.float32)
    for i, (g, off) in enumerate(steps):
        window, dil = C_PAIRS[g]
        delta = off * tq + col - r
        ok = (np.abs(delta) <= window // 2) & (delta % dil == 0)
        out[i] = np.where(ok, 0.0, MASK_BIAS)
    return out


def _dilated_attention(qkv, *, batch, seq, tq=C_TQ):
    steps = _dilated_steps(tq)
    n_steps = len(steps)
    grp = jnp.asarray([g for g, _ in steps], jnp.int32)
    off = jnp.asarray([o for _, o in steps], jnp.int32)
    bias = jnp.asarray(_dilated_bias(tq))
    nq = seq // tq
    hw = C_HEADS * C_HEAD_DIM
    c = (C_HEAD_DIM ** -0.5) * LOG2E
    dh = C_HEAD_DIM

    def body(grp_ref, off_ref, q_ref, k_ref, v_ref, b_ref, o_ref, m_ref, l_ref, acc_ref):
        qi = pl.program_id(1)
        st = pl.program_id(2)

        @pl.when(st == 0)
        def _():
            m_ref[...] = jnp.full(m_ref.shape, -jnp.inf, F32)
            l_ref[...] = jnp.zeros(l_ref.shape, F32)
            acc_ref[...] = jnp.zeros(acc_ref.shape, F32)

        kb = qi + off_ref[st]

        @pl.when((kb >= 0) & (kb < nq))
        def _():
            bias_t = b_ref[0]
            for h in range(C_HEADS):
                sl = slice(h * dh, (h + 1) * dh)
                s = lax.dot_general(q_ref[:, sl], k_ref[:, sl], (((1,), (1,)), ((), ())),
                                    preferred_element_type=F32) * c + bias_t
                m_prev = m_ref[h]
                m_new = jnp.maximum(m_prev, jnp.max(s, axis=1, keepdims=True))
                alpha = jnp.exp2(m_prev - m_new)
                p = jnp.exp2(s - m_new[:, :1])
                l_ref[h] = alpha * l_ref[h] + jnp.sum(p, axis=1, keepdims=True)
                acc_ref[h] = acc_ref[h] * alpha + jnp.dot(
                    p.astype(BF16), v_ref[:, sl], preferred_element_type=F32)
                m_ref[h] = m_new

        @pl.when(st == n_steps - 1)
        def _():
            for h in range(C_HEADS):
                o_ref[:, h * dh:(h + 1) * dh] = (acc_ref[h] / l_ref[h]).astype(o_ref.dtype)

    def kv_block(b, qi, st, grp_ref, off_ref):
        return b * nq + jnp.clip(qi + off_ref[st], 0, nq - 1)

    grid_spec = pltpu.PrefetchScalarGridSpec(
        num_scalar_prefetch=2,
        grid=(batch, nq, n_steps),
        in_specs=[
            pl.BlockSpec((tq, hw), lambda b, qi, st, g_r, o_r: (b * nq + qi, g_r[st])),
            pl.BlockSpec((tq, hw), lambda b, qi, st, g_r, o_r:
                         (kv_block(b, qi, st, g_r, o_r), C_GROUPS + g_r[st])),
            pl.BlockSpec((tq, hw), lambda b, qi, st, g_r, o_r:
                         (kv_block(b, qi, st, g_r, o_r), 2 * C_GROUPS + g_r[st])),
            pl.BlockSpec((1, tq, tq), lambda b, qi, st, g_r, o_r: (st, 0, 0)),
        ],
        out_specs=pl.BlockSpec((tq, hw), lambda b, qi, st, g_r, o_r: (b * nq + qi, 0)),
        scratch_shapes=[
            pltpu.VMEM((C_HEADS, tq, LANES), F32),
            pltpu.VMEM((C_HEADS, tq, LANES), F32),
            pltpu.VMEM((C_HEADS, tq, dh), F32),
        ],
    )
    return pl.pallas_call(
        body,
        grid_spec=grid_spec,
        out_shape=jax.ShapeDtypeStruct((batch * seq, hw), BF16),
        compiler_params=_params(3),
    )(grp, off, qkv, qkv, qkv, bias)


def _mlp_residual(x, gain, w_in, w_out, *, tm=512, tf=1024):
    t_rows, d = x.shape
    nf = w_in.shape[1] // tf

    def body(x_ref, g_ref, wi_ref, wo_ref, o_ref, xn_ref, acc_ref):
        f = pl.program_id(1)

        @pl.when(f == 0)
        def _():
            _rms_to_bf16(x_ref, g_ref, xn_ref)

        h = jnp.dot(xn_ref[...], wi_ref[...], preferred_element_type=F32)
        h = jnp.square(jnp.maximum(h, 0.0)).astype(BF16)
        part = jnp.dot(h, wo_ref[...], preferred_element_type=F32)

        @pl.when(f == 0)
        def _():
            acc_ref[...] = part

        @pl.when(f > 0)
        def _():
            acc_ref[...] += part

        @pl.when(f == nf - 1)
        def _():
            o_ref[...] = x_ref[...] + acc_ref[...]

    return pl.pallas_call(
        body,
        grid=(t_rows // tm, nf),
        in_specs=[
            pl.BlockSpec((tm, d), lambda i, f: (i, 0)),
            pl.BlockSpec((1, d), lambda i, f: (0, 0)),
            pl.BlockSpec((d, tf), lambda i, f: (0, f)),
            pl.BlockSpec((tf, d), lambda i, f: (f, 0)),
        ],
        out_specs=pl.BlockSpec((tm, d), lambda i, f: (i, 0)),
        out_shape=jax.ShapeDtypeStruct((t_rows, d), F32),
        scratch_shapes=[pltpu.VMEM((tm, d), BF16), pltpu.VMEM((tm, d), F32)],
        compiler_params=_params(2),
    )(x, gain, w_in, w_out)


def _ple_residual(x, p, gate_g, w_gate, w_ple, ple_g, *, tm=256):
    t_rows, d = x.shape
    pd = p.shape[1]

    def body(x_ref, p_ref, gg_ref, wg_ref, wp_ref, pg_ref, o_ref, xn_ref):
        _rms_to_bf16(x_ref, gg_ref, xn_ref)
        e = jnp.dot(p_ref[...].astype(BF16), wp_ref[...], preferred_element_type=F32)
        ms = jnp.mean(e * e, axis=-1, keepdims=True)
        e = e * lax.rsqrt(ms + EPS) * pg_ref[...]
        z = jnp.dot(xn_ref[...], wg_ref[...], preferred_element_type=F32)
        o_ref[...] = x_ref[...] + jax.nn.sigmoid(z) * e

    const = lambda i: (0, 0)
    return pl.pallas_call(
        body,
        grid=(t_rows // tm,),
        in_specs=[
            pl.BlockSpec((tm, d), lambda i: (i, 0)),
            pl.BlockSpec((tm, pd), lambda i: (i, 0)),
            pl.BlockSpec((1, d), const),
            pl.BlockSpec((d, d), const),
            pl.BlockSpec((pd, d), const),
            pl.BlockSpec((1, d), const),
        ],
        out_specs=pl.BlockSpec((tm, d), lambda i: (i, 0)),
        out_shape=jax.ShapeDtypeStruct((t_rows, d), F32),
        scratch_shapes=[pltpu.VMEM((tm, d), BF16)],
        compiler_params=_params(1),
    )(x, p, gate_g, w_gate, w_ple, ple_g)


def _rope_cos_sin(pos, dim, theta):
    inv_freq = theta ** (-jnp.arange(0, dim, 2, dtype=F32) / dim)
    ang = pos.astype(F32)[:, None] * inv_freq[None, :]
    return jnp.cos(ang), jnp.sin(ang)


def _axial_tables(seq):
    t = jnp.arange(seq)
    half = A_HEAD_DIM // 2
    cr, sr = _rope_cos_sin(t // GRID_W, half, A_ROPE_THETA)
    cc, sc = _rope_cos_sin(t % GRID_W, half, A_ROPE_THETA)
    return (jnp.concatenate([cr, cr, cc, cc], axis=-1),
            jnp.concatenate([-sr, sr, -sc, sc], axis=-1))


def _partial_tables(seq, rope_dim, theta):
    c, s = _rope_cos_sin(jnp.arange(seq), rope_dim, theta)
    rest = LANES - rope_dim
    return (jnp.concatenate([c, c, jnp.ones((seq, rest), F32)], axis=-1),
            jnp.concatenate([-s, s, jnp.zeros((seq, rest), F32)], axis=-1))


def _row(v):
    return v.reshape(1, -1).astype(F32)


def _prep_weights(w):
    out = {}
    out["w_mlp_in"] = w["w_mlp_in"].astype(BF16)
    out["w_mlp_out"] = w["w_mlp_out"].astype(BF16)
    out["w_ple"] = w["w_ple"].astype(BF16)
    out["w_ple_gate"] = w["w_ple_gate"].astype(BF16)
    out["a_w_qkv"] = w["a_w_qkv"].astype(BF16)
    out["a_w_o"] = w["a_w_o"].astype(BF16)
    out["a_head_g"] = jnp.concatenate(
        [jnp.tile(w["a_q_norm_g"], (1, A_HEADS)), jnp.tile(w["a_k_norm_g"], (1, A_KV_HEADS)),
         jnp.ones((w["a_q_norm_g"].shape[0], A_KV_HEADS * A_HEAD_DIM), F32)], axis=-1)

    n_b = w["b_w_dqkv"].shape[0]
    lat_w = B_Q_RANK + B_KV_RANK + B_ROPE_DIM
    out["b_w_dqkv"] = jnp.pad(w["b_w_dqkv"], ((0, 0), (0, 0), (0, B_LAT_PAD - lat_w))).astype(BF16)
    uq = w["b_w_uq"].reshape(n_b, B_Q_RANK, B_HEADS, B_QK_DIM)
    uq = jnp.pad(uq, ((0, 0), (0, 0), (0, 0), (0, B_HEAD_PAD - B_QK_DIM)))
    out["b_w_uq"] = uq.reshape(n_b, B_Q_RANK, B_HEADS * B_HEAD_PAD).astype(BF16)
    ukv = w["b_w_ukv"].reshape(n_b, B_KV_RANK, B_HEADS, B_NOPE_DIM + B_V_DIM)
    out["b_w_uk"] = ukv[..., :B_NOPE_DIM].reshape(n_b, B_KV_RANK, B_HEADS * B_NOPE_DIM).astype(BF16)
    out["b_w_uv"] = ukv[..., B_NOPE_DIM:].reshape(n_b, B_KV_RANK, B_HEADS * B_V_DIM).astype(BF16)
    pad_g = ((0, 0), (0, B_HEAD_PAD - B_QK_DIM))
    out["b_q_g"] = jnp.pad(w["b_q_norm_g"], pad_g)
    out["b_k_g"] = jnp.pad(w["b_k_norm_g"], pad_g)
    out["b_w_o"] = w["b_w_o"].astype(BF16)

    out["c_w_qkv"] = w["c_w_qkv"].astype(BF16)
    out["c_w_o"] = w["c_w_o"].astype(BF16)
    n_ch = C_GROUPS * C_HEADS
    out["c_head_g"] = jnp.concatenate(
        [jnp.tile(w["c_q_norm_g"], (1, n_ch)), jnp.tile(w["c_k_norm_g"], (1, n_ch)),
         jnp.ones((w["c_q_norm_g"].shape[0], n_ch * C_HEAD_DIM), F32)], axis=-1)
    return out


def _mixer_a(x, norm_g, w, pw, j, batch, seq):
    cos, sin = _axial_tables(seq)
    qkv = _qkv_proj(x, pw["a_w_qkv"][j], norm_g, _row(pw["a_head_g"][j]), cos, sin, seq=seq,
                    n_normed_tiles=(A_HEADS + A_KV_HEADS) * A_HEAD_DIM // 512,
                    half=A_HEAD_DIM // 4)
    group = A_HEADS // A_KV_HEADS
    o = _dense_attention(qkv, qkv, qkv, batch=batch, seq=seq, n_groups=A_KV_HEADS,
                         heads=group, kv_heads=1, dq=A_HEAD_DIM,
                         q_cb0=0, k_cb0=A_HEADS, v_cb0=A_HEADS + A_KV_HEADS,
                         scale=A_HEAD_DIM ** -0.5)
    return _out_proj_residual(o, pw["a_w_o"][j], x)


def _mixer_b(x, norm_g, w, pw, j, batch, seq):
    cos, sin = _partial_tables(seq, B_ROPE_DIM, B_ROPE_THETA)
    tm = 1024
    n_pos_blocks = seq // tm
    lat = _fused_matmul(x, pw["b_w_dqkv"][j], tm=tm, tn=B_LAT_PAD // 3, gain=norm_g, out_dtype=F32)

    def pos_spec():
        return pl.BlockSpec((tm, LANES), lambda i, jj: (i % n_pos_blocks, 0))

    head_g_spec = pl.BlockSpec((1, B_HEAD_PAD), lambda i, jj: (0, 0))
    tn_q = 4 * B_HEAD_PAD
    q = _fused_matmul(lat, pw["b_w_uq"][j], tm=tm, tn=tn_q, gain=_row(w["b_cq_norm_g"][j]),
                      x_kblock=0,
                      extras=[(_row(pw["b_q_g"][j]), head_g_spec), (cos, pos_spec()), (sin, pos_spec())],
                      epilogue=_mla_q_epilogue(tn_q))
    tn_k = 4 * B_NOPE_DIM
    ckv_g = _row(w["b_ckv_norm_g"][j])
    k_rope_block = (B_Q_RANK + B_KV_RANK) // LANES
    k = _fused_matmul(lat, pw["b_w_uk"][j], tm=tm, tn=tn_k, gain=ckv_g, x_kblock=1,
                      extras=[(lat, pl.BlockSpec((tm, LANES), lambda i, jj: (i, k_rope_block))),
                              (_row(pw["b_k_g"][j]), head_g_spec), (cos, pos_spec()), (sin, pos_spec())],
                      epilogue=_mla_k_epilogue(tn_k), out_tile=4 * B_HEAD_PAD)
    v = _fused_matmul(lat, pw["b_w_uv"][j], tm=tm, tn=512, gain=ckv_g, x_kblock=1)
    heads = 4
    o = _dense_attention(q, k, v, batch=batch, seq=seq, n_groups=B_HEADS // heads,
                         heads=heads, kv_heads=heads, dq=B_HEAD_PAD,
                         q_cb0=0, k_cb0=0, v_cb0=0, scale=B_QK_DIM ** -0.5)
    return _out_proj_residual(o, pw["b_w_o"][j], x)


def _mixer_c(x, norm_g, w, pw, j, batch, seq):
    cos, sin = _partial_tables(seq, C_ROPE_DIM, C_ROPE_THETA)
    n_ch = C_GROUPS * C_HEADS
    qkv = _qkv_proj(x, pw["c_w_qkv"][j], norm_g, _row(pw["c_head_g"][j]), cos, sin, seq=seq,
                    n_normed_tiles=2 * n_ch * C_HEAD_DIM // 512, half=C_ROPE_DIM // 2)
    o = _dilated_attention(qkv, batch=batch, seq=seq)
    return _out_proj_residual(o, pw["c_w_o"][j], x)


def _layer_stack(x, p, w, pw):
    batch, seq, d = x.shape
    x = x.reshape(batch * seq, d)
    p = p.reshape(DEPTH, batch * seq, PLE_DIM)
    for i in range(DEPTH):
        kind, j = i % N_MIXERS, i // N_MIXERS
        mixer = (_mixer_a, _mixer_b, _mixer_c)[kind]
        x = mixer(x, _row(w["norm_mix_g"][i]), w, pw, j, batch, seq)
        x = _mlp_residual(x, _row(w["norm_mlp_g"][i]), pw["w_mlp_in"][i], pw["w_mlp_out"][i])
        x = _ple_residual(x, p[i], _row(w["ple_gate_norm_g"][i]), pw["w_ple_gate"][i],
                          pw["w_ple"][i], _row(w["ple_norm_g"][i]))
    return x.reshape(batch, seq, d)


def kernel(x_prompt, x_sample, p_prompt, p_sample, norm_mix_g, norm_mlp_g, w_mlp_in, w_mlp_out, w_ple, ple_norm_g, ple_gate_norm_g, w_ple_gate, a_w_qkv, a_q_norm_g, a_k_norm_g, a_w_o, b_w_dqkv, b_cq_norm_g, b_ckv_norm_g, b_w_uq, b_w_ukv, b_q_norm_g, b_k_norm_g, b_w_o, c_w_qkv, c_q_norm_g, c_k_norm_g, c_w_o):
    w = dict(norm_mix_g=norm_mix_g, norm_mlp_g=norm_mlp_g, w_mlp_in=w_mlp_in, w_mlp_out=w_mlp_out,
             w_ple=w_ple, ple_norm_g=ple_norm_g, ple_gate_norm_g=ple_gate_norm_g,
             w_ple_gate=w_ple_gate, a_w_qkv=a_w_qkv, a_q_norm_g=a_q_norm_g, a_k_norm_g=a_k_norm_g,
             a_w_o=a_w_o, b_w_dqkv=b_w_dqkv, b_cq_norm_g=b_cq_norm_g, b_ckv_norm_g=b_ckv_norm_g,
             b_w_uq=b_w_uq, b_w_ukv=b_w_ukv, b_q_norm_g=b_q_norm_g, b_k_norm_g=b_k_norm_g,
             b_w_o=b_w_o, c_w_qkv=c_w_qkv, c_q_norm_g=c_q_norm_g, c_k_norm_g=c_k_norm_g, c_w_o=c_w_o)
    pw = _prep_weights(w)
    return (_layer_stack(x_prompt, p_prompt, w, pw), _layer_stack(x_sample, p_sample, w, pw))
```

```python
import functools
import math

import numpy as np
import jax
import jax.numpy as jnp
from jax import lax
from jax.experimental import pallas as pl
from jax.experimental.pallas import tpu as pltpu

F32 = jnp.float32
BF16 = jnp.bfloat16

D_MODEL = 2048
DEPTH = 4
N_MIXERS = 3
PLE_DIM = 256
GRID_W = 64
EPS = 1e-6
D_FF = 4 * D_MODEL

A_HEADS = 16
A_KV_HEADS = 4
A_HEAD_DIM = 128
A_ROPE_THETA = 10000.0

B_HEADS = 16
B_Q_RANK = 512
B_KV_RANK = 512
B_NOPE_DIM = 128
B_ROPE_DIM = 64
B_V_DIM = 128
B_QK_DIM = B_NOPE_DIM + B_ROPE_DIM
B_ROPE_THETA = 10000.0
B_HEAD_PAD = 256
B_LAT_PAD = 1152

C_PAIRS = ((128, 1), (512, 4), (2048, 16))
C_GROUPS = 3
C_HEADS = 8
C_HEAD_DIM = 128
C_ROPE_DIM = 32
C_ROPE_THETA = 500000.0

LANES = 128
LOG2E = math.log2(math.e)
MASK_BIAS = -1e30
VMEM_LIMIT = 56 * 1024 * 1024


def _params(n_axes):
    return pltpu.CompilerParams(dimension_semantics=("arbitrary",) * n_axes,
                                vmem_limit_bytes=VMEM_LIMIT)


def _rms_to_bf16(x_ref, g_ref, xn_ref, rows=128):
    tm = x_ref.shape[0]

    def chunk(c, carry):
        r0 = pl.multiple_of(c * rows, rows)
        xf = x_ref[pl.ds(r0, rows), :]
        ms = jnp.mean(xf * xf, axis=-1, keepdims=True)
        xn_ref[pl.ds(r0, rows), :] = (xf * lax.rsqrt(ms + EPS) * g_ref[...]).astype(BF16)
        return carry

    lax.fori_loop(0, tm // rows, chunk, 0)


def _rope(y, cos, sin):
    return y * cos + pltpu.roll(y, LANES // 2, 1) * sin


def _fused_matmul(x, w, *, tm, tn, chunk, gain=None, x_kblock=0, extras=(), epilogue=None,
                  n_epilogue_cols=None, out_tile=None, out_dtype=BF16):
    t_rows = x.shape[0]
    k_dim, n_dim = w.shape
    nj = n_dim // tn
    ow = out_tile or tn
    norm = gain is not None
    n_extra = len(extras)
    n_epi = 0 if epilogue is None else (n_dim if n_epilogue_cols is None else n_epilogue_cols)
    assert nj == 1 or n_epi % tn == 0
    assert n_epi == n_dim or ow == tn

    def body(*refs):
        x_ref, w_ref = refs[0], refs[1]
        pos = 2
        if norm:
            g_ref = refs[pos]
            pos += 1
        extra_refs = refs[pos:pos + n_extra]
        o_ref = refs[pos + n_extra]
        j = pl.program_id(1)
        if norm:
            lhs_ref = refs[pos + n_extra + 1]

            @pl.when(j == 0)
            def _():
                _rms_to_bf16(x_ref, g_ref, lhs_ref)
        else:
            lhs_ref = x_ref

        def run(with_epilogue):
            for c0 in range(0, tn, chunk):
                acc = jnp.dot(lhs_ref[...], w_ref[:, c0:c0 + chunk], preferred_element_type=F32)
                if with_epilogue(c0):
                    epilogue(acc, c0, extra_refs, o_ref)
                else:
                    o_ref[:, c0:c0 + chunk] = acc.astype(o_ref.dtype)

        if nj == 1:
            run(lambda c0: c0 < n_epi)
        elif n_epi in (0, n_dim):
            run(lambda c0: n_epi > 0)
        else:
            @pl.when(j < n_epi // tn)
            def _():
                run(lambda c0: True)

            @pl.when(j >= n_epi // tn)
            def _():
                run(lambda c0: False)

    in_specs = [pl.BlockSpec((tm, k_dim), lambda i, j: (i, x_kblock)),
                pl.BlockSpec((k_dim, tn), lambda i, j: (0, j))]
    args = [x, w]
    if norm:
        in_specs.append(pl.BlockSpec((1, k_dim), lambda i, j: (0, 0)))
        args.append(gain)
    for arr, spec in extras:
        in_specs.append(spec)
        args.append(arr)
    scratch = [pltpu.VMEM((tm, k_dim), BF16)] if norm else []
    return pl.pallas_call(
        body,
        grid=(t_rows // tm, nj),
        in_specs=in_specs,
        out_specs=pl.BlockSpec((tm, ow), lambda i, j: (i, j)),
        out_shape=jax.ShapeDtypeStruct((t_rows, nj * ow), out_dtype),
        scratch_shapes=scratch,
        compiler_params=_params(2),
    )(*args)


def _residual_epilogue(acc, c0, extra_refs, o_ref):
    (res_ref,) = extra_refs
    cols = slice(c0, c0 + acc.shape[1])
    o_ref[:, cols] = res_ref[:, cols] + acc


def _out_proj_residual(o, w_o, x, *, tm=512):
    n_dim = w_o.shape[1]
    extras = [(x, pl.BlockSpec((tm, n_dim), lambda i, j: (i, 0)))]
    return _fused_matmul(o, w_o, tm=tm, tn=n_dim, chunk=512, extras=extras,
                         epilogue=_residual_epilogue, out_dtype=F32)


def _head_norm_rope_epilogue(acc, c0, extra_refs, o_ref):
    g_ref, cos_ref, sin_ref = extra_refs
    for h in range(acc.shape[1] // LANES):
        c = acc[:, h * LANES:(h + 1) * LANES]
        cols = slice(c0 + h * LANES, c0 + (h + 1) * LANES)
        ms = jnp.mean(c * c, axis=-1, keepdims=True)
        y = c * lax.rsqrt(ms + EPS) * g_ref[:, cols]
        o_ref[:, cols] = _rope(y, cos_ref[...], sin_ref[...]).astype(o_ref.dtype)


def _qkv_proj(x, w, norm_g, head_g, cos, sin, *, seq, n_normed_cols, tm, tn):
    n_pos_blocks = seq // tm
    extras = [
        (head_g, pl.BlockSpec((1, tn), lambda i, j: (0, j))),
        (cos, pl.BlockSpec((tm, LANES), lambda i, j: (i % n_pos_blocks, 0))),
        (sin, pl.BlockSpec((tm, LANES), lambda i, j: (i % n_pos_blocks, 0))),
    ]
    return _fused_matmul(x, w, tm=tm, tn=tn, chunk=512, gain=norm_g, extras=extras,
                         epilogue=_head_norm_rope_epilogue, n_epilogue_cols=n_normed_cols)


def _mla_q_epilogue(acc, c0, extra_refs, o_ref):
    g_ref, cos_ref, sin_ref = extra_refs
    for h in range(acc.shape[1] // B_HEAD_PAD):
        c_lo = acc[:, h * B_HEAD_PAD:h * B_HEAD_PAD + LANES]
        c_hi = acc[:, h * B_HEAD_PAD + LANES:(h + 1) * B_HEAD_PAD]
        ss = (jnp.sum(c_lo * c_lo, axis=-1, keepdims=True)
              + jnp.sum(c_hi * c_hi, axis=-1, keepdims=True)) * (1.0 / B_QK_DIM)
        r = lax.rsqrt(ss + EPS)
        base = c0 + h * B_HEAD_PAD
        o_ref[:, base:base + LANES] = (c_lo * r * g_ref[:, :LANES]).astype(o_ref.dtype)
        y_hi = c_hi * r * g_ref[:, LANES:]
        o_ref[:, base + LANES:base + B_HEAD_PAD] = _rope(
            y_hi, cos_ref[...], sin_ref[...]).astype(o_ref.dtype)


def _mla_k_epilogue(acc, c0, extra_refs, o_ref):
    kr_ref, g_ref, cos_ref, sin_ref = extra_refs
    kr = kr_ref[...]
    kr_ss = jnp.sum(kr * kr, axis=-1, keepdims=True)
    for h in range(acc.shape[1] // LANES):
        c = acc[:, h * LANES:(h + 1) * LANES]
        ss = (jnp.sum(c * c, axis=-1, keepdims=True) + kr_ss) * (1.0 / B_QK_DIM)
        r = lax.rsqrt(ss + EPS)
        base = (c0 // LANES + h) * B_HEAD_PAD
        o_ref[:, base:base + LANES] = (c * r * g_ref[:, :LANES]).astype(o_ref.dtype)
        y_hi = kr * r * g_ref[:, LANES:]
        o_ref[:, base + LANES:base + B_HEAD_PAD] = _rope(
            y_hi, cos_ref[...], sin_ref[...]).astype(o_ref.dtype)


def _dense_attention(q_arr, k_arr, vt_arr, *, batch, seq, n_groups, heads, kv_heads, dq,
                     q_cb0, k_cb0, tq=512, tk=1024, tks=512, rc=64):
    nq, nk = seq // tq, seq // tk
    dv = LANES

    def body(q_ref, k_ref, vt_ref, o_ref, qt_ref, m_ref, l_ref, acc_ref, s_ref, p_ref):
        ki = pl.program_id(3)

        @pl.when(ki == 0)
        def _():
            for g in range(heads):
                qg = q_ref[:, g * dq:(g + 1) * dq].astype(F32)
                qt_ref[g] = qg.T.astype(BF16)
            m_ref[...] = jnp.full(m_ref.shape, -jnp.inf, F32)
            l_ref[...] = jnp.zeros(l_ref.shape, F32)
            acc_ref[...] = jnp.zeros(acc_ref.shape, F32)

        units = [(g, sb) for sb in range(tk // tks) for g in range(heads)]

        def scores(unit):
            g, sb = unit
            kv = g * kv_heads // heads
            k = k_ref[sb * tks:(sb + 1) * tks, kv * dq:(kv + 1) * dq]
            return jnp.dot(k, qt_ref[g], preferred_element_type=F32)

        s_ref[0] = scores(units[0])
        for i, (g, sb) in enumerate(units):
            slot = i % 2
            if i + 1 < len(units):
                s_ref[1 - slot] = scores(units[i + 1])
            kv = g * kv_heads // heads
            vt = vt_ref[kv * dv:(kv + 1) * dv, sb * tks:(sb + 1) * tks]
            m_prev = m_ref[g]
            m_new = jnp.maximum(m_prev, jnp.max(s_ref[slot], axis=0, keepdims=True))
            alpha = jnp.exp2(m_prev - m_new)
            l_part = jnp.zeros((8, tq), F32)
            for r in range(tks // rc):
                rows = slice(r * rc, (r + 1) * rc)
                pc = jnp.exp2(s_ref[slot, rows, :] - m_new)
                l_part = l_part + jnp.sum(pc.reshape(rc // 8, 8, tq), axis=0)
                p_ref[slot, rows, :] = pc.astype(BF16)
            l_ref[g] = alpha * l_ref[g] + jnp.sum(l_part, axis=0, keepdims=True)
            pv = jnp.dot(vt, p_ref[slot], preferred_element_type=F32)
            acc_ref[g] = acc_ref[g] * alpha + pv
            m_ref[g] = m_new

        @pl.when(ki == nk - 1)
        def _():
            for g in range(heads):
                o = acc_ref[g] / l_ref[g]
                o_ref[:, g * dv:(g + 1) * dv] = o.T.astype(o_ref.dtype)

    t_rows = batch * seq
    return pl.pallas_call(
        body,
        grid=(batch, n_groups, nq, nk),
        in_specs=[
            pl.BlockSpec((tq, heads * dq), lambda b, g, qi, ki: (b * nq + qi, q_cb0 + g)),
            pl.BlockSpec((tk, kv_heads * dq), lambda b, g, qi, ki: (b * nk + ki, k_cb0 + g)),
            pl.BlockSpec((kv_heads * dv, tk), lambda b, g, qi, ki: (g, b * nk + ki)),
        ],
        out_specs=pl.BlockSpec((tq, heads * dv), lambda b, g, qi, ki: (b * nq + qi, g)),
        out_shape=jax.ShapeDtypeStruct((t_rows, n_groups * heads * dv), BF16),
        scratch_shapes=[
            pltpu.VMEM((heads, dq, tq), BF16),
            pltpu.VMEM((heads, 1, tq), F32),
            pltpu.VMEM((heads, 1, tq), F32),
            pltpu.VMEM((heads, dv, tq), F32),
            pltpu.VMEM((2, tks, tq), F32),
            pltpu.VMEM((2, tks, tq), BF16),
        ],
        compiler_params=_params(4),
    )(q_arr, k_arr, vt_arr)


C_TL = 256
C_REACH = 64
assert all(window // (2 * dil) == C_REACH for window, dil in C_PAIRS)
assert tuple(dil for _, dil in C_PAIRS) == (1, 4, 16)


def _band_bias(tl):
    i = np.arange(tl)[None, :]
    jh = np.arange(C_REACH)[:, None]
    jc = np.arange(tl)[:, None]

    def as_bias(ok):
        return jnp.asarray(np.where(ok, 0.0, MASK_BIAS), F32)

    return as_bias(jh >= i), as_bias(np.abs(jc - i) <= C_REACH), as_bias(i >= tl - C_REACH + jh)


def _to_residue_major(a, batch, seq, dil):
    c = a.shape[-1]
    return a.reshape(batch, seq // dil, dil, c).transpose(0, 2, 1, 3).reshape(batch * seq, c)


def _from_residue_major(a, batch, seq, dil):
    c = a.shape[-1]
    return a.reshape(batch, dil, seq // dil, c).transpose(0, 2, 1, 3).reshape(batch * seq, c)


def _banded_attention(x, *, rows_per_group, seq, tl=C_TL):
    hw = C_HEADS * C_HEAD_DIM
    dh = C_HEAD_DIM
    n_rows = x.shape[0]
    blocks_per_group = rows_per_group // tl
    halo_per_block = tl // C_REACH
    n_halo_blocks = n_rows // C_REACH
    seq_blocks = [max(seq // dil // tl, 1) for _, dil in C_PAIRS]
    assert all(seq // dil >= tl for _, dil in C_PAIRS)
    bias_prev, bias_cur, bias_next = _band_bias(tl)
    nt_dims = (((1,), (1,)), ((), ()))
    tn_dims = (((0,), (0,)), ((), ()))

    def body(q_ref, kp_ref, kc_ref, kn_ref, vp_ref, vc_ref, vn_ref, bp_ref, bc_ref, bn_ref,
             o_ref, lse_ref):
        blk = pl.program_id(0)
        grp = blk // blocks_per_group
        nb_seq = jnp.where(grp == 0, seq_blocks[0], jnp.where(grp == 1, seq_blocks[1], seq_blocks[2]))
        pos = (blk % blocks_per_group) % nb_seq
        b_prev = bp_ref[...] + jnp.where(pos == 0, MASK_BIAS, 0.0)
        b_next = bn_ref[...] + jnp.where(pos == nb_seq - 1, MASK_BIAS, 0.0)
        b_cur = bc_ref[...]
        for h in range(C_HEADS):
            sl = slice(h * dh, (h + 1) * dh)
            q = q_ref[:, sl]
            s_p = lax.dot_general(kp_ref[:, sl], q, nt_dims, preferred_element_type=F32) + b_prev
            s_c = lax.dot_general(kc_ref[:, sl], q, nt_dims, preferred_element_type=F32) + b_cur
            s_n = lax.dot_general(kn_ref[:, sl], q, nt_dims, preferred_element_type=F32) + b_next
            m = jnp.maximum(jnp.max(s_c, axis=0, keepdims=True),
                            jnp.maximum(jnp.max(s_p, axis=0, keepdims=True),
                                        jnp.max(s_n, axis=0, keepdims=True)))
            p_p = jnp.exp2(s_p - m)
            p_c = jnp.exp2(s_c - m)
            p_n = jnp.exp2(s_n - m)
            l = (jnp.sum(p_c, axis=0, keepdims=True) + jnp.sum(p_p, axis=0, keepdims=True)
                 + jnp.sum(p_n, axis=0, keepdims=True))
            ot = (lax.dot_general(vc_ref[:, sl], p_c.astype(BF16), tn_dims, preferred_element_type=F32)
                  + lax.dot_general(vp_ref[:, sl], p_p.astype(BF16), tn_dims, preferred_element_type=F32)
                  + lax.dot_general(vn_ref[:, sl], p_n.astype(BF16), tn_dims, preferred_element_type=F32))
            o_ref[:, sl] = (ot / l).T
            lse_ref[h:h + 1, :] = (m + jnp.log2(l)) * (1.0 / LOG2E)

    def prev_halo(blk):
        return jnp.maximum(blk * halo_per_block - 1, 0)

    def next_halo(blk):
        return jnp.minimum((blk + 1) * halo_per_block, n_halo_blocks - 1)

    const = lambda blk: (0, 0)
    return pl.pallas_call(
        body,
        grid=(n_rows // tl,),
        in_specs=[
            pl.BlockSpec((tl, hw), lambda blk: (blk, 0)),
            pl.BlockSpec((C_REACH, hw), lambda blk: (prev_halo(blk), 1)),
            pl.BlockSpec((tl, hw), lambda blk: (blk, 1)),
            pl.BlockSpec((C_REACH, hw), lambda blk: (next_halo(blk), 1)),
            pl.BlockSpec((C_REACH, hw), lambda blk: (prev_halo(blk), 2)),
            pl.BlockSpec((tl, hw), lambda blk: (blk, 2)),
            pl.BlockSpec((C_REACH, hw), lambda blk: (next_halo(blk), 2)),
            pl.BlockSpec((C_REACH, tl), const),
            pl.BlockSpec((tl, tl), const),
            pl.BlockSpec((C_REACH, tl), const),
        ],
        out_specs=[pl.BlockSpec((tl, hw), lambda blk: (blk, 0)),
                   pl.BlockSpec((C_HEADS, tl), lambda blk: (0, blk))],
        out_shape=[jax.ShapeDtypeStruct((n_rows, hw), F32),
                   jax.ShapeDtypeStruct((C_HEADS, n_rows), F32)],
        compiler_params=_params(1),
    )(x, x, x, x, x, x, x, bias_prev, bias_cur, bias_next)


def _merge_out_proj_residual(o3, lse3, w_o, x, *, tm=512, tn=512):
    t_rows = x.shape[0]
    hw, n_dim = w_o.shape
    dh = C_HEAD_DIM

    def body(o_ref, lse_ref, w_ref, res_ref, out_ref, mg_ref):
        @pl.when(pl.program_id(1) == 0)
        def _():
            lse = [lse_ref[g] for g in range(C_GROUPS)]
            mx = jnp.maximum(lse[0], jnp.maximum(lse[1], lse[2]))
            e = [jnp.exp(v - mx) for v in lse]
            den = e[0] + e[1] + e[2]
            wts = [v / den for v in e]
            for h in range(C_HEADS):
                sl = slice(h * dh, (h + 1) * dh)
                mg = (wts[0][:, h:h + 1] * o_ref[0, :, sl] + wts[1][:, h:h + 1] * o_ref[1, :, sl]
                      + wts[2][:, h:h + 1] * o_ref[2, :, sl])
                mg_ref[:, sl] = mg.astype(BF16)

        out_ref[...] = res_ref[...] + jnp.dot(mg_ref[...], w_ref[...], preferred_element_type=F32)

    return pl.pallas_call(
        body,
        grid=(t_rows // tm, n_dim // tn),
        in_specs=[
            pl.BlockSpec((C_GROUPS, tm, hw), lambda i, j: (0, i, 0)),
            pl.BlockSpec((C_GROUPS, tm, C_HEADS), lambda i, j: (0, i, 0)),
            pl.BlockSpec((hw, tn), lambda i, j: (0, j)),
            pl.BlockSpec((tm, tn), lambda i, j: (i, j)),
        ],
        out_specs=pl.BlockSpec((tm, tn), lambda i, j: (i, j)),
        out_shape=jax.ShapeDtypeStruct((t_rows, n_dim), F32),
        scratch_shapes=[pltpu.VMEM((tm, hw), BF16)],
        compiler_params=_params(2),
    )(o3, lse3, w_o, x)


def _mlp_residual(x, gain, w_in, w_out, *, tm=512, tf=1024):
    t_rows, d = x.shape
    nf = w_in.shape[1] // tf

    def body(x_ref, g_ref, wi_ref, wo_ref, o_ref, xn_ref, acc_ref):
        f = pl.program_id(1)

        @pl.when(f == 0)
        def _():
            _rms_to_bf16(x_ref, g_ref, xn_ref)
            acc_ref[...] = jnp.zeros(acc_ref.shape, F32)

        h = jnp.dot(xn_ref[...], wi_ref[...], preferred_element_type=F32)
        h = jnp.square(jnp.maximum(h, 0.0)).astype(BF16)
        acc_ref[...] += jnp.dot(h, wo_ref[...], preferred_element_type=F32)

        @pl.when(f == nf - 1)
        def _():
            o_ref[...] = x_ref[...] + acc_ref[...]

    return pl.pallas_call(
        body,
        grid=(t_rows // tm, nf),
        in_specs=[
            pl.BlockSpec((tm, d), lambda i, f: (i, 0)),
            pl.BlockSpec((1, d), lambda i, f: (0, 0)),
            pl.BlockSpec((d, tf), lambda i, f: (0, f)),
            pl.BlockSpec((tf, d), lambda i, f: (f, 0)),
        ],
        out_specs=pl.BlockSpec((tm, d), lambda i, f: (i, 0)),
        out_shape=jax.ShapeDtypeStruct((t_rows, d), F32),
        scratch_shapes=[pltpu.VMEM((tm, d), BF16), pltpu.VMEM((tm, d), F32)],
        compiler_params=_params(2),
    )(x, gain, w_in, w_out)


def _ple_residual(x, p, gate_g, w_gate, w_ple, ple_g, *, tm=256):
    t_rows, d = x.shape
    pd = p.shape[1]

    def body(x_ref, p_ref, gg_ref, wg_ref, wp_ref, pg_ref, o_ref, xn_ref):
        _rms_to_bf16(x_ref, gg_ref, xn_ref)
        e = jnp.dot(p_ref[...].astype(BF16), wp_ref[...], preferred_element_type=F32)
        ms = jnp.mean(e * e, axis=-1, keepdims=True)
        e = e * lax.rsqrt(ms + EPS) * pg_ref[...]
        z = jnp.dot(xn_ref[...], wg_ref[...], preferred_element_type=F32)
        o_ref[...] = x_ref[...] + jax.nn.sigmoid(z) * e

    const = lambda i: (0, 0)
    return pl.pallas_call(
        body,
        grid=(t_rows // tm,),
        in_specs=[
            pl.BlockSpec((tm, d), lambda i: (i, 0)),
            pl.BlockSpec((tm, pd), lambda i: (i, 0)),
            pl.BlockSpec((1, d), const),
            pl.BlockSpec((d, d), const),
            pl.BlockSpec((pd, d), const),
            pl.BlockSpec((1, d), const),
        ],
        out_specs=pl.BlockSpec((tm, d), lambda i: (i, 0)),
        out_shape=jax.ShapeDtypeStruct((t_rows, d), F32),
        scratch_shapes=[pltpu.VMEM((tm, d), BF16)],
        compiler_params=_params(1),
    )(x, p, gate_g, w_gate, w_ple, ple_g)


def _rope_cos_sin(pos, dim, theta):
    inv_freq = theta ** (-jnp.arange(0, dim, 2, dtype=F32) / dim)
    ang = pos.astype(F32)[:, None] * inv_freq[None, :]
    return jnp.cos(ang), jnp.sin(ang)


def _pair_perm(first, second, rest):
    fill = LANES // 2 - len(first)
    perm = list(first) + list(rest[:fill]) + list(second) + list(rest[fill:])
    assert sorted(perm) == list(range(LANES))
    return np.asarray(perm)


A_PERM = _pair_perm(list(range(0, 32)) + list(range(64, 96)),
                    list(range(32, 64)) + list(range(96, 128)), [])
B_PERM = _pair_perm(range(0, 32), range(32, 64), list(range(64, 128)))
C_PERM = _pair_perm(range(0, 16), range(16, 32), list(range(32, 128)))


def _permute_heads(w_cols, n_heads, perm):
    lead = w_cols.shape[:-1]
    n_cols = n_heads * LANES
    heads = w_cols[..., :n_cols].reshape(lead + (n_heads, LANES))[..., perm]
    return jnp.concatenate([heads.reshape(lead + (n_cols,)), w_cols[..., n_cols:]], axis=-1)


def _axial_tables(seq):
    t = jnp.arange(seq)
    half = A_HEAD_DIM // 2
    cr, sr = _rope_cos_sin(t // GRID_W, half, A_ROPE_THETA)
    cc, sc = _rope_cos_sin(t % GRID_W, half, A_ROPE_THETA)
    return (jnp.concatenate([cr, cc, cr, cc], axis=-1),
            jnp.concatenate([-sr, -sc, sr, sc], axis=-1))


def _partial_tables(seq, rope_dim, theta):
    c, s = _rope_cos_sin(jnp.arange(seq), rope_dim, theta)
    fill = LANES // 2 - rope_dim // 2
    one, zero = jnp.ones((seq, fill), F32), jnp.zeros((seq, fill), F32)
    return (jnp.concatenate([c, one, c, one], axis=-1),
            jnp.concatenate([-s, zero, s, zero], axis=-1))


def _row(v):
    return v.reshape(1, -1).astype(F32)


def _prep_weights(w):
    out = {}
    out["w_mlp_in"] = w["w_mlp_in"].astype(BF16)
    out["w_mlp_out"] = w["w_mlp_out"].astype(BF16)
    out["w_ple"] = w["w_ple"].astype(BF16)
    out["w_ple_gate"] = w["w_ple_gate"].astype(BF16)
    out["a_w_qkv"] = _permute_heads(w["a_w_qkv"], A_HEADS + A_KV_HEADS, A_PERM).astype(BF16)
    out["a_w_o"] = w["a_w_o"].astype(BF16)
    out["a_head_g"] = jnp.concatenate(
        [jnp.tile(w["a_q_norm_g"][:, A_PERM] * (A_HEAD_DIM ** -0.5 * LOG2E), (1, A_HEADS)),
         jnp.tile(w["a_k_norm_g"][:, A_PERM], (1, A_KV_HEADS)),
         jnp.ones((w["a_q_norm_g"].shape[0], A_KV_HEADS * A_HEAD_DIM), F32)], axis=-1)

    n_b = w["b_w_dqkv"].shape[0]
    lat_w = B_Q_RANK + B_KV_RANK + B_ROPE_DIM
    n_lat_groups = B_LAT_PAD // LANES

    def permute_last_group(a, n_groups):
        lead = a.shape[:-1]
        grouped = a.reshape(lead + (n_groups, LANES))
        last = grouped[..., n_groups - 1:, :][..., B_PERM]
        return jnp.concatenate([grouped[..., :n_groups - 1, :], last], axis=-2).reshape(a.shape)

    dqkv = jnp.pad(w["b_w_dqkv"], ((0, 0), (0, 0), (0, B_LAT_PAD - lat_w)))
    out["b_w_dqkv"] = permute_last_group(dqkv, n_lat_groups).astype(BF16)
    uq = w["b_w_uq"].reshape(n_b, B_Q_RANK, B_HEADS, B_QK_DIM)
    uq = jnp.pad(uq, ((0, 0), (0, 0), (0, 0), (0, B_HEAD_PAD - B_QK_DIM)))
    uq = permute_last_group(uq, B_HEAD_PAD // LANES)
    out["b_w_uq"] = uq.reshape(n_b, B_Q_RANK, B_HEADS * B_HEAD_PAD).astype(BF16)
    ukv = w["b_w_ukv"].reshape(n_b, B_KV_RANK, B_HEADS, B_NOPE_DIM + B_V_DIM)
    out["b_w_uk"] = ukv[..., :B_NOPE_DIM].reshape(n_b, B_KV_RANK, B_HEADS * B_NOPE_DIM).astype(BF16)
    out["b_w_uv"] = ukv[..., B_NOPE_DIM:].reshape(n_b, B_KV_RANK, B_HEADS * B_V_DIM).astype(BF16)
    pad_g = ((0, 0), (0, B_HEAD_PAD - B_QK_DIM))
    out["b_q_g"] = permute_last_group(
        jnp.pad(w["b_q_norm_g"] * (B_QK_DIM ** -0.5 * LOG2E), pad_g), B_HEAD_PAD // LANES)
    out["b_k_g"] = permute_last_group(jnp.pad(w["b_k_norm_g"], pad_g), B_HEAD_PAD // LANES)
    out["b_w_o"] = w["b_w_o"].astype(BF16)

    n_ch = C_GROUPS * C_HEADS
    out["c_w_qkv"] = _permute_heads(w["c_w_qkv"], 2 * n_ch, C_PERM).astype(BF16)
    out["c_w_o"] = w["c_w_o"].astype(BF16)
    out["c_head_g"] = jnp.concatenate(
        [jnp.tile(w["c_q_norm_g"][:, C_PERM] * (C_HEAD_DIM ** -0.5 * LOG2E), (1, n_ch)),
         jnp.tile(w["c_k_norm_g"][:, C_PERM], (1, n_ch)),
         jnp.ones((w["c_q_norm_g"].shape[0], n_ch * C_HEAD_DIM), F32)], axis=-1)
    return out


def _mixer_a(x, norm_g, w, pw, j, batch, seq):
    cos, sin = _axial_tables(seq)
    w_qkv = pw["a_w_qkv"][j]
    qkv = _qkv_proj(x, w_qkv, norm_g, _row(pw["a_head_g"][j]), cos, sin, seq=seq,
                    n_normed_cols=(A_HEADS + A_KV_HEADS) * A_HEAD_DIM, tm=512, tn=w_qkv.shape[1])
    group = A_HEADS // A_KV_HEADS
    vt = qkv[:, (A_HEADS + A_KV_HEADS) * A_HEAD_DIM:].T
    o = _dense_attention(qkv, qkv, vt, batch=batch, seq=seq, n_groups=A_KV_HEADS,
                         heads=group, kv_heads=1, dq=A_HEAD_DIM,
                         q_cb0=0, k_cb0=A_HEADS)
    return _out_proj_residual(o, pw["a_w_o"][j], x)


def _mixer_b(x, norm_g, w, pw, j, batch, seq):
    cos, sin = _partial_tables(seq, B_ROPE_DIM, B_ROPE_THETA)
    tm = 512
    n_pos_blocks = seq // tm
    lat = _fused_matmul(x, pw["b_w_dqkv"][j], tm=tm, tn=B_LAT_PAD, chunk=B_LAT_PAD // 3,
                        gain=norm_g, out_dtype=F32)

    def pos_spec():
        return pl.BlockSpec((tm, LANES), lambda i, jj: (i % n_pos_blocks, 0))

    head_g_spec = pl.BlockSpec((1, B_HEAD_PAD), lambda i, jj: (0, 0))
    q = _fused_matmul(lat, pw["b_w_uq"][j], tm=tm, tn=B_HEADS * B_HEAD_PAD, chunk=512,
                      gain=_row(w["b_cq_norm_g"][j]), x_kblock=0,
                      extras=[(_row(pw["b_q_g"][j]), head_g_spec), (cos, pos_spec()), (sin, pos_spec())],
                      epilogue=_mla_q_epilogue)
    ckv_g = _row(w["b_ckv_norm_g"][j])
    k_rope_block = (B_Q_RANK + B_KV_RANK) // LANES
    k = _fused_matmul(lat, pw["b_w_uk"][j], tm=tm, tn=B_HEADS * B_NOPE_DIM, chunk=512,
                      gain=ckv_g, x_kblock=1,
                      extras=[(lat, pl.BlockSpec((tm, LANES), lambda i, jj: (i, k_rope_block))),
                              (_row(pw["b_k_g"][j]), head_g_spec), (cos, pos_spec()), (sin, pos_spec())],
                      epilogue=_mla_k_epilogue, out_tile=B_HEADS * B_HEAD_PAD)
    v = _fused_matmul(lat, pw["b_w_uv"][j], tm=tm, tn=B_HEADS * B_V_DIM, chunk=512,
                      gain=ckv_g, x_kblock=1)
    heads = 4
    o = _dense_attention(q, k, v.T, batch=batch, seq=seq, n_groups=B_HEADS // heads,
                         heads=heads, kv_heads=heads, dq=B_HEAD_PAD,
                         q_cb0=0, k_cb0=0)
    return _out_proj_residual(o, pw["b_w_o"][j], x)


def _mixer_c(x, norm_g, w, pw, j, batch, seq):
    cos, sin = _partial_tables(seq, C_ROPE_DIM, C_ROPE_THETA)
    n_ch = C_GROUPS * C_HEADS
    qkv = _qkv_proj(x, pw["c_w_qkv"][j], norm_g, _row(pw["c_head_g"][j]), cos, sin, seq=seq,
                    n_normed_cols=2 * n_ch * C_HEAD_DIM, tm=1024, tn=1536)
    t_rows = batch * seq
    hw = C_HEADS * C_HEAD_DIM
    by_group = qkv.reshape(t_rows, 3, C_GROUPS, hw)
    xs = jnp.concatenate(
        [_to_residue_major(by_group[:, :, g, :].reshape(t_rows, 3 * hw), batch, seq, dil)
         for g, (_, dil) in enumerate(C_PAIRS)], axis=0)
    o, lse = _banded_attention(xs, rows_per_group=t_rows, seq=seq)
    o3 = jnp.stack([_from_residue_major(o[g * t_rows:(g + 1) * t_rows], batch, seq, dil)
                    for g, (_, dil) in enumerate(C_PAIRS)])
    lse3 = jnp.stack([_from_residue_major(lse[:, g * t_rows:(g + 1) * t_rows].T, batch, seq, dil)
                      for g, (_, dil) in enumerate(C_PAIRS)])
    return _merge_out_proj_residual(o3, lse3, pw["c_w_o"][j], x)


def _layer_stack(x, p, w, pw):
    batch, seq, d = x.shape
    x = x.reshape(batch * seq, d)
    p = p.reshape(DEPTH, batch * seq, PLE_DIM)
    for i in range(DEPTH):
        kind, j = i % N_MIXERS, i // N_MIXERS
        mixer = (_mixer_a, _mixer_b, _mixer_c)[kind]
        x = mixer(x, _row(w["norm_mix_g"][i]), w, pw, j, batch, seq)
        x = _mlp_residual(x, _row(w["norm_mlp_g"][i]), pw["w_mlp_in"][i], pw["w_mlp_out"][i])
        x = _ple_residual(x, p[i], _row(w["ple_gate_norm_g"][i]), pw["w_ple_gate"][i],
                          pw["w_ple"][i], _row(w["ple_norm_g"][i]))
    return x.reshape(batch, seq, d)


def kernel(x_prompt, x_sample, p_prompt, p_sample, norm_mix_g, norm_mlp_g, w_mlp_in, w_mlp_out, w_ple, ple_norm_g, ple_gate_norm_g, w_ple_gate, a_w_qkv, a_q_norm_g, a_k_norm_g, a_w_o, b_w_dqkv, b_cq_norm_g, b_ckv_norm_g, b_w_uq, b_w_ukv, b_q_norm_g, b_k_norm_g, b_w_o, c_w_qkv, c_q_norm_g, c_k_norm_g, c_w_o):
    w = dict(norm_mix_g=norm_mix_g, norm_mlp_g=norm_mlp_g, w_mlp_in=w_mlp_in, w_mlp_out=w_mlp_out,
             w_ple=w_ple, ple_norm_g=ple_norm_g, ple_gate_norm_g=ple_gate_norm_g,
             w_ple_gate=w_ple_gate, a_w_qkv=a_w_qkv, a_q_norm_g=a_q_norm_g, a_k_norm_g=a_k_norm_g,
             a_w_o=a_w_o, b_w_dqkv=b_w_dqkv, b_cq_norm_g=b_cq_norm_g, b_ckv_norm_g=b_ckv_norm_g,
             b_w_uq=b_w_uq, b_w_ukv=b_w_ukv, b_q_norm_g=b_q_norm_g, b_k_norm_g=b_k_norm_g,
             b_w_o=b_w_o, c_w_qkv=c_w_qkv, c_q_norm_g=c_q_norm_g, c_k_norm_g=c_k_norm_g, c_w_o=c_w_o)
    pw = _prep_weights(w)
    return (_layer_stack(x_prompt, p_prompt, w, pw), _layer_stack(x_sample, p_sample, w, pw))
```

```python
import functools
import math

import numpy as np
import jax
import jax.numpy as jnp
from jax import lax
from jax.experimental import pallas as pl
from jax.experimental.pallas import tpu as pltpu

F32 = jnp.float32
BF16 = jnp.bfloat16

D_MODEL = 2048
DEPTH = 4
N_MIXERS = 3
PLE_DIM = 256
GRID_W = 64
EPS = 1e-6
D_FF = 4 * D_MODEL

A_HEADS = 16
A_KV_HEADS = 4
A_HEAD_DIM = 128
A_ROPE_THETA = 10000.0

B_HEADS = 16
B_Q_RANK = 512
B_KV_RANK = 512
B_NOPE_DIM = 128
B_ROPE_DIM = 64
B_V_DIM = 128
B_QK_DIM = B_NOPE_DIM + B_ROPE_DIM
B_ROPE_THETA = 10000.0
B_HEAD_PAD = 256
B_LAT_PAD = 1152

C_PAIRS = ((128, 1), (512, 4), (2048, 16))
C_GROUPS = 3
C_HEADS = 8
C_HEAD_DIM = 128
C_ROPE_DIM = 32
C_ROPE_THETA = 500000.0

LANES = 128
DEN_ROWS = 16
LOG2E = math.log2(math.e)
MASK_BIAS = -1e30
VMEM_LIMIT = 56 * 1024 * 1024


def _params(n_axes):
    return pltpu.CompilerParams(dimension_semantics=("arbitrary",) * n_axes,
                                vmem_limit_bytes=VMEM_LIMIT)


def _rms_to_bf16(x_ref, g_ref, xn_ref, rows=128):
    tm = x_ref.shape[0]

    def chunk(c, carry):
        r0 = pl.multiple_of(c * rows, rows)
        xf = x_ref[pl.ds(r0, rows), :]
        ms = jnp.mean(xf * xf, axis=-1, keepdims=True)
        xn_ref[pl.ds(r0, rows), :] = (xf * lax.rsqrt(ms + EPS) * g_ref[...]).astype(BF16)
        return carry

    lax.fori_loop(0, tm // rows, chunk, 0)


def _rope(y, cos, sin):
    return y * cos + pltpu.roll(y, LANES // 2, 1) * sin


def _fused_matmul(x, w, *, tm, tn, chunk, gain=None, x_kblock=0, extras=(), epilogue=None,
                  n_epilogue_cols=None, out_tile=None, out_dtype=BF16, n_transposed_cols=0):
    t_rows = x.shape[0]
    k_dim, n_dim = w.shape
    nj = n_dim // tn
    ow = out_tile or tn
    norm = gain is not None
    n_extra = len(extras)
    n_epi = 0 if epilogue is None else (n_dim if n_epilogue_cols is None else n_epilogue_cols)
    assert nj == 1 or n_epi % tn == 0
    assert n_epi == n_dim or ow == tn
    n_t = n_transposed_cols
    n_main = n_dim - n_t
    assert n_t == 0 or (nj == 1 and n_t % chunk == 0 and n_epi <= n_main and ow == tn)

    def body(*refs):
        x_ref, w_ref = refs[0], refs[1]
        pos = 2
        if norm:
            g_ref = refs[pos]
            pos += 1
        extra_refs = refs[pos:pos + n_extra]
        pos += n_extra
        o_ref = t_ref = None
        if n_main:
            o_ref = refs[pos]
            pos += 1
        if n_t:
            t_ref = refs[pos]
            pos += 1
        j = pl.program_id(1)
        if norm:
            lhs_ref = refs[pos]

            @pl.when(j == 0)
            def _():
                _rms_to_bf16(x_ref, g_ref, lhs_ref)
        else:
            lhs_ref = x_ref

        def run(with_epilogue):
            for c0 in range(0, tn, chunk):
                acc = jnp.dot(lhs_ref[...], w_ref[:, c0:c0 + chunk], preferred_element_type=F32)
                if c0 >= n_main:
                    t_ref[c0 - n_main:c0 - n_main + chunk, :] = acc.T.astype(BF16)
                elif with_epilogue(c0):
                    epilogue(acc, c0, extra_refs, o_ref)
                else:
                    o_ref[:, c0:c0 + chunk] = acc.astype(o_ref.dtype)

        if nj == 1:
            run(lambda c0: c0 < n_epi)
        elif n_epi in (0, n_dim):
            run(lambda c0: n_epi > 0)
        else:
            @pl.when(j < n_epi // tn)
            def _():
                run(lambda c0: True)

            @pl.when(j >= n_epi // tn)
            def _():
                run(lambda c0: False)

    in_specs = [pl.BlockSpec((tm, k_dim), lambda i, j: (i, x_kblock)),
                pl.BlockSpec((k_dim, tn), lambda i, j: (0, j))]
    args = [x, w]
    if norm:
        in_specs.append(pl.BlockSpec((1, k_dim), lambda i, j: (0, 0)))
        args.append(gain)
    for arr, spec in extras:
        in_specs.append(spec)
        args.append(arr)
    scratch = [pltpu.VMEM((tm, k_dim), BF16)] if norm else []
    out_specs, out_shape = [], []
    if n_main:
        main_w = ow if n_t == 0 else n_main
        out_specs.append(pl.BlockSpec((tm, main_w), lambda i, j: (i, j)))
        out_shape.append(jax.ShapeDtypeStruct((t_rows, nj * main_w), out_dtype))
    if n_t:
        out_specs.append(pl.BlockSpec((n_t, tm), lambda i, j: (0, i)))
        out_shape.append(jax.ShapeDtypeStruct((n_t, t_rows), BF16))
    if len(out_specs) == 1:
        out_specs, out_shape = out_specs[0], out_shape[0]
    return pl.pallas_call(
        body,
        grid=(t_rows // tm, nj),
        in_specs=in_specs,
        out_specs=out_specs,
        out_shape=out_shape,
        scratch_shapes=scratch,
        compiler_params=_params(2),
    )(*args)


def _residual_epilogue(acc, c0, extra_refs, o_ref):
    (res_ref,) = extra_refs
    cols = slice(c0, c0 + acc.shape[1])
    o_ref[:, cols] = res_ref[:, cols] + acc


def _out_proj_residual(o, w_o, x, *, tm=512):
    n_dim = w_o.shape[1]
    extras = [(x, pl.BlockSpec((tm, n_dim), lambda i, j: (i, 0)))]
    return _fused_matmul(o, w_o, tm=tm, tn=n_dim, chunk=512, extras=extras,
                         epilogue=_residual_epilogue, out_dtype=F32)


def _head_norm_rope(c, gain, cos, sin):
    ms = jnp.mean(c * c, axis=-1, keepdims=True)
    return _rope(c * lax.rsqrt(ms + EPS) * gain, cos, sin)


def _head_norm_rope_epilogue(acc, c0, extra_refs, o_ref):
    g_ref, cos_ref, sin_ref = extra_refs
    for h in range(acc.shape[1] // LANES):
        cols = slice(c0 + h * LANES, c0 + (h + 1) * LANES)
        y = _head_norm_rope(acc[:, h * LANES:(h + 1) * LANES], g_ref[:, cols],
                            cos_ref[...], sin_ref[...])
        o_ref[:, cols] = y.astype(o_ref.dtype)


def _qkv_proj(x, w, norm_g, head_g, cos, sin, *, seq, n_normed_cols, tm, tn, n_transposed_cols=0):
    n_pos_blocks = seq // tm
    extras = [
        (head_g, pl.BlockSpec((1, tn), lambda i, j: (0, j))),
        (cos, pl.BlockSpec((tm, LANES), lambda i, j: (i % n_pos_blocks, 0))),
        (sin, pl.BlockSpec((tm, LANES), lambda i, j: (i % n_pos_blocks, 0))),
    ]
    return _fused_matmul(x, w, tm=tm, tn=tn, chunk=512, gain=norm_g, extras=extras,
                         epilogue=_head_norm_rope_epilogue, n_epilogue_cols=n_normed_cols,
                         n_transposed_cols=n_transposed_cols)


def _mla_q_epilogue(acc, c0, extra_refs, o_ref):
    g_ref, cos_ref, sin_ref = extra_refs
    for h in range(acc.shape[1] // B_HEAD_PAD):
        c_lo = acc[:, h * B_HEAD_PAD:h * B_HEAD_PAD + LANES]
        c_hi = acc[:, h * B_HEAD_PAD + LANES:(h + 1) * B_HEAD_PAD]
        ss = (jnp.sum(c_lo * c_lo, axis=-1, keepdims=True)
              + jnp.sum(c_hi * c_hi, axis=-1, keepdims=True)) * (1.0 / B_QK_DIM)
        r = lax.rsqrt(ss + EPS)
        base = c0 + h * B_HEAD_PAD
        o_ref[:, base:base + LANES] = (c_lo * r * g_ref[:, :LANES]).astype(o_ref.dtype)
        y_hi = c_hi * r * g_ref[:, LANES:]
        o_ref[:, base + LANES:base + B_HEAD_PAD] = _rope(
            y_hi, cos_ref[...], sin_ref[...]).astype(o_ref.dtype)


def _mla_k_epilogue(acc, c0, extra_refs, o_ref):
    kr_ref, g_ref, cos_ref, sin_ref = extra_refs
    kr = kr_ref[...]
    kr_ss = jnp.sum(kr * kr, axis=-1, keepdims=True)
    for h in range(acc.shape[1] // LANES):
        c = acc[:, h * LANES:(h + 1) * LANES]
        ss = (jnp.sum(c * c, axis=-1, keepdims=True) + kr_ss) * (1.0 / B_QK_DIM)
        r = lax.rsqrt(ss + EPS)
        base = (c0 // LANES + h) * B_HEAD_PAD
        o_ref[:, base:base + LANES] = (c * r * g_ref[:, :LANES]).astype(o_ref.dtype)
        y_hi = kr * r * g_ref[:, LANES:]
        o_ref[:, base + LANES:base + B_HEAD_PAD] = _rope(
            y_hi, cos_ref[...], sin_ref[...]).astype(o_ref.dtype)


def _dense_attention(q_arr, k_arr, vt_arr, *, batch, seq, n_groups, heads, kv_heads, dq,
                     q_cb0, k_cb0, tq=512, tk=512):
    nq, nk = seq // tq, seq // tk
    dv = LANES

    def body(q_ref, k_ref, vt_ref, o_ref, qt_ref, m_ref, acc_ref, s_ref, mc_ref):
        ki = pl.program_id(3)

        def score_phase(slot):
            for g in range(heads):
                kv = g * kv_heads // heads
                s = jnp.dot(k_ref[:, kv * dq:(kv + 1) * dq], qt_ref[g],
                            preferred_element_type=F32)
                s_ref[slot, g] = s
                mc_ref[slot, g] = jnp.max(s, axis=0, keepdims=True)

        def value_phase(slot):
            ones_rows = jnp.ones((DEN_ROWS, tk), BF16)
            for g in range(heads):
                kv = g * kv_heads // heads
                m_prev = m_ref[g]
                m_new = jnp.maximum(m_prev, mc_ref[slot, g])
                alpha = jnp.exp2(m_prev - m_new)
                p = jnp.exp2(s_ref[slot, g] - m_new).astype(BF16)
                vt_ext = jnp.concatenate([vt_ref[kv * dv:(kv + 1) * dv, :], ones_rows], axis=0)
                pv = jnp.dot(vt_ext, p, preferred_element_type=F32)
                acc_ref[g] = acc_ref[g] * alpha + pv
                m_ref[g] = m_new

        @pl.when(ki == 0)
        def _():
            for g in range(heads):
                qg = q_ref[:, g * dq:(g + 1) * dq].astype(F32)
                qt_ref[g] = qg.T.astype(BF16)
            m_ref[...] = jnp.full(m_ref.shape, -jnp.inf, F32)
            acc_ref[...] = jnp.zeros(acc_ref.shape, F32)
            score_phase(0)

        for parity in (0, 1):
            @pl.when((ki > 0) & (ki < nk) & (ki % 2 == parity))
            def _():
                score_phase(parity)
                value_phase(1 - parity)

        @pl.when(ki == nk)
        def _():
            value_phase((nk - 1) % 2)
            for g in range(heads):
                o = acc_ref[g, :dv, :] / acc_ref[g, dv:dv + 1, :]
                o_ref[:, g * dv:(g + 1) * dv] = o.T.astype(o_ref.dtype)

    t_rows = batch * seq
    return pl.pallas_call(
        body,
        grid=(batch, n_groups, nq, nk + 1),
        in_specs=[
            pl.BlockSpec((tq, heads * dq), lambda b, g, qi, ki: (b * nq + qi, q_cb0 + g)),
            pl.BlockSpec((tk, kv_heads * dq),
                         lambda b, g, qi, ki: (b * nk + jnp.minimum(ki, nk - 1), k_cb0 + g)),
            pl.BlockSpec((kv_heads * dv, tk),
                         lambda b, g, qi, ki: (g, b * nk + jnp.maximum(ki - 1, 0))),
        ],
        out_specs=pl.BlockSpec((tq, heads * dv), lambda b, g, qi, ki: (b * nq + qi, g)),
        out_shape=jax.ShapeDtypeStruct((t_rows, n_groups * heads * dv), BF16),
        scratch_shapes=[
            pltpu.VMEM((heads, dq, tq), BF16),
            pltpu.VMEM((heads, 1, tq), F32),
            pltpu.VMEM((heads, dv + DEN_ROWS, tq), F32),
            pltpu.VMEM((2, heads, tk, tq), F32),
            pltpu.VMEM((2, heads, 1, tq), F32),
        ],
        compiler_params=_params(4),
    )(q_arr, k_arr, vt_arr)


C_TL = 256
C_REACH = 64
assert all(window // (2 * dil) == C_REACH for window, dil in C_PAIRS)
assert tuple(dil for _, dil in C_PAIRS) == (1, 4, 16)


def _band_bias(tl):
    i = np.arange(tl)[None, :]
    jh = np.arange(C_REACH)[:, None]
    jc = np.arange(tl)[:, None]

    def as_bias(ok):
        return jnp.asarray(np.where(ok, 0.0, MASK_BIAS), F32)

    return as_bias(jh >= i), as_bias(np.abs(jc - i) <= C_REACH), as_bias(i >= tl - C_REACH + jh)


def _dilated_qkv_proj(x, w, norm_g, head_g, cos, sin, *, batch, seq, tm=1024, chunk=512):
    t_rows, k_dim = x.shape
    hw = C_HEADS * C_HEAD_DIM
    n_steps = 3 * C_GROUPS
    blocks_per_seq = seq // tm
    dils = [dil for _, dil in C_PAIRS]
    assert all(tm % dil == 0 and (tm // dil) % 16 == 0 for dil in dils)

    def body(x_ref, w_ref, g_ref, hg_ref, cos_ref, sin_ref, o0_ref, o1_ref, o2_ref, xn_ref, res_ref):
        j = pl.program_id(1)

        @pl.when(j == 0)
        def _():
            _rms_to_bf16(x_ref, g_ref, xn_ref)

        def matmul(with_epilogue):
            for c0 in range(0, hw, chunk):
                acc = jnp.dot(xn_ref[...], w_ref[:, c0:c0 + chunk], preferred_element_type=F32)
                for h in range(chunk // LANES):
                    c = acc[:, h * LANES:(h + 1) * LANES]
                    head = c0 // LANES + h
                    if with_epilogue:
                        c = _head_norm_rope(c, hg_ref[:, head * LANES:(head + 1) * LANES],
                                            cos_ref[...], sin_ref[...])
                    res_ref[head] = c

        @pl.when(j % 3 < 2)
        def _():
            matmul(True)

        @pl.when(j % 3 == 2)
        def _():
            matmul(False)

        for g, (o_ref, dil) in enumerate(zip((o0_ref, o1_ref, o2_ref), dils)):
            @pl.when(j // 3 == g)
            def _():
                for r in range(dil):
                    for head in range(C_HEADS):
                        o_ref[r, :, head * LANES:(head + 1) * LANES] = res_ref[
                            head, pl.ds(r, tm // dil, stride=dil), :].astype(BF16)

    def out_spec(g, dil):
        return pl.BlockSpec(
            (dil, tm // dil, hw),
            lambda i, j: (i // blocks_per_seq, i % blocks_per_seq, jnp.clip(j - 3 * g, 0, 2)))

    return pl.pallas_call(
        body,
        grid=(t_rows // tm, n_steps),
        in_specs=[
            pl.BlockSpec((tm, k_dim), lambda i, j: (i, 0)),
            pl.BlockSpec((k_dim, hw), lambda i, j: (0, j)),
            pl.BlockSpec((1, k_dim), lambda i, j: (0, 0)),
            pl.BlockSpec((1, hw), lambda i, j: (0, j)),
            pl.BlockSpec((tm, LANES), lambda i, j: (i % blocks_per_seq, 0)),
            pl.BlockSpec((tm, LANES), lambda i, j: (i % blocks_per_seq, 0)),
        ],
        out_specs=[out_spec(g, dil) for g, dil in enumerate(dils)],
        out_shape=[jax.ShapeDtypeStruct((batch * dil, seq // dil, 3 * hw), BF16) for dil in dils],
        scratch_shapes=[pltpu.VMEM((tm, k_dim), BF16), pltpu.VMEM((C_HEADS, tm, LANES), F32)],
        compiler_params=_params(2),
    )(x, w, norm_g, head_g, cos, sin)


def _banded_attention(x, *, seq_len, tl=C_TL):
    hw = C_HEADS * C_HEAD_DIM
    dh = C_HEAD_DIM
    n_rows = x.shape[0]
    halo_per_block = tl // C_REACH
    n_halo_blocks = n_rows // C_REACH
    nb_seq = seq_len // tl
    assert seq_len % tl == 0
    bias_prev, bias_cur, bias_next = _band_bias(tl)
    nt_dims = (((1,), (1,)), ((), ()))
    tn_dims = (((0,), (0,)), ((), ()))

    def body(q_ref, kp_ref, kc_ref, kn_ref, vp_ref, vc_ref, vn_ref, bp_ref, bc_ref, bn_ref,
             o_ref, lse_ref):
        pos = pl.program_id(0) % nb_seq
        b_prev = bp_ref[...] + jnp.where(pos == 0, MASK_BIAS, 0.0)
        b_next = bn_ref[...] + jnp.where(pos == nb_seq - 1, MASK_BIAS, 0.0)
        b_cur = bc_ref[...]
        lse_rows = []
        for h in range(C_HEADS):
            sl = slice(h * dh, (h + 1) * dh)
            q = q_ref[:, sl]
            s_p = lax.dot_general(kp_ref[:, sl], q, nt_dims, preferred_element_type=F32) + b_prev
            s_c = lax.dot_general(kc_ref[:, sl], q, nt_dims, preferred_element_type=F32) + b_cur
            s_n = lax.dot_general(kn_ref[:, sl], q, nt_dims, preferred_element_type=F32) + b_next
            m = jnp.maximum(jnp.max(s_c, axis=0, keepdims=True),
                            jnp.maximum(jnp.max(s_p, axis=0, keepdims=True),
                                        jnp.max(s_n, axis=0, keepdims=True)))
            p_p = jnp.exp2(s_p - m)
            p_c = jnp.exp2(s_c - m)
            p_n = jnp.exp2(s_n - m)
            l = (jnp.sum(p_c, axis=0, keepdims=True) + jnp.sum(p_p, axis=0, keepdims=True)
                 + jnp.sum(p_n, axis=0, keepdims=True))
            ot = (lax.dot_general(vc_ref[:, sl], p_c.astype(BF16), tn_dims, preferred_element_type=F32)
                  + lax.dot_general(vp_ref[:, sl], p_p.astype(BF16), tn_dims, preferred_element_type=F32)
                  + lax.dot_general(vn_ref[:, sl], p_n.astype(BF16), tn_dims, preferred_element_type=F32))
            o_ref[:, sl] = (ot / l).T
            lse_rows.append((m + jnp.log2(l)) * (1.0 / LOG2E))
        lse_t = jnp.concatenate(lse_rows + [jnp.zeros((LANES - C_HEADS, tl), F32)], axis=0)
        lse_ref[...] = lse_t.T

    def prev_halo(blk):
        return jnp.maximum(blk * halo_per_block - 1, 0)

    def next_halo(blk):
        return jnp.minimum((blk + 1) * halo_per_block, n_halo_blocks - 1)

    const = lambda blk: (0, 0)
    return pl.pallas_call(
        body,
        grid=(n_rows // tl,),
        in_specs=[
            pl.BlockSpec((tl, hw), lambda blk: (blk, 0)),
            pl.BlockSpec((C_REACH, hw), lambda blk: (prev_halo(blk), 1)),
            pl.BlockSpec((tl, hw), lambda blk: (blk, 1)),
            pl.BlockSpec((C_REACH, hw), lambda blk: (next_halo(blk), 1)),
            pl.BlockSpec((C_REACH, hw), lambda blk: (prev_halo(blk), 2)),
            pl.BlockSpec((tl, hw), lambda blk: (blk, 2)),
            pl.BlockSpec((C_REACH, hw), lambda blk: (next_halo(blk), 2)),
            pl.BlockSpec((C_REACH, tl), const),
            pl.BlockSpec((tl, tl), const),
            pl.BlockSpec((C_REACH, tl), const),
        ],
        out_specs=[pl.BlockSpec((tl, hw), lambda blk: (blk, 0)),
                   pl.BlockSpec((tl, LANES), lambda blk: (blk, 0))],
        out_shape=[jax.ShapeDtypeStruct((n_rows, hw), F32),
                   jax.ShapeDtypeStruct((n_rows, LANES), F32)],
        compiler_params=_params(1),
    )(x, x, x, x, x, x, x, bias_prev, bias_cur, bias_next)


def _merge_out_proj_residual(os, lses, w_o, x, *, batch, seq, tm=512, tn=512):
    t_rows = x.shape[0]
    hw, n_dim = w_o.shape
    dh = C_HEAD_DIM
    dils = [dil for _, dil in C_PAIRS]
    blocks_per_seq = seq // tm
    assert all(tm % dil == 0 and (tm // dil) % 8 == 0 for dil in dils)

    def body(o0_ref, o1_ref, o2_ref, l0_ref, l1_ref, l2_ref, w_ref, res_ref, out_ref,
             ot_ref, lt_ref, mg_ref):
        @pl.when(pl.program_id(1) == 0)
        def _():
            for g, (o_ref, l_ref, dil) in enumerate(zip((o0_ref, o1_ref, o2_ref),
                                                        (l0_ref, l1_ref, l2_ref), dils)):
                for r in range(dil):
                    rows = pl.ds(r, tm // dil, stride=dil)
                    for h in range(C_HEADS):
                        ot_ref[g, h, rows, :] = o_ref[r, :, h * dh:(h + 1) * dh]
                    lt_ref[g, rows, :] = l_ref[r]
            lse = [lt_ref[g] for g in range(C_GROUPS)]
            mx = jnp.maximum(lse[0], jnp.maximum(lse[1], lse[2]))
            e = [jnp.exp(v - mx) for v in lse]
            den = e[0] + e[1] + e[2]
            wts = [v / den for v in e]
            for h in range(C_HEADS):
                sl = slice(h * dh, (h + 1) * dh)
                mg = (wts[0][:, h:h + 1] * ot_ref[0, h] + wts[1][:, h:h + 1] * ot_ref[1, h]
                      + wts[2][:, h:h + 1] * ot_ref[2, h])
                mg_ref[:, sl] = mg.astype(BF16)

        out_ref[...] = res_ref[...] + jnp.dot(mg_ref[...], w_ref[...], preferred_element_type=F32)

    def group_spec(dil, width):
        return pl.BlockSpec((dil, tm // dil, width),
                            lambda i, j: (i // blocks_per_seq, i % blocks_per_seq, 0))

    args = ([o.reshape(batch * dil, seq // dil, hw) for o, dil in zip(os, dils)]
            + [l.reshape(batch * dil, seq // dil, LANES) for l, dil in zip(lses, dils)])
    return pl.pallas_call(
        body,
        grid=(t_rows // tm, n_dim // tn),
        in_specs=([group_spec(dil, hw) for dil in dils] + [group_spec(dil, LANES) for dil in dils]
                  + [pl.BlockSpec((hw, tn), lambda i, j: (0, j)),
                     pl.BlockSpec((tm, tn), lambda i, j: (i, j))]),
        out_specs=pl.BlockSpec((tm, tn), lambda i, j: (i, j)),
        out_shape=jax.ShapeDtypeStruct((t_rows, n_dim), F32),
        scratch_shapes=[pltpu.VMEM((C_GROUPS, C_HEADS, tm, dh), F32),
                        pltpu.VMEM((C_GROUPS, tm, LANES), F32),
                        pltpu.VMEM((tm, hw), BF16)],
        compiler_params=_params(2),
    )(*args, w_o, x)


def _mlp_residual(x, gain, w_in, w_out, *, tm=512, tf=1024):
    t_rows, d = x.shape
    nf = w_in.shape[1] // tf

    def body(x_ref, g_ref, wi_ref, wo_ref, o_ref, xn_ref, acc_ref):
        f = pl.program_id(1)

        @pl.when(f == 0)
        def _():
            _rms_to_bf16(x_ref, g_ref, xn_ref)
            acc_ref[...] = jnp.zeros(acc_ref.shape, F32)

        h = jnp.dot(xn_ref[...], wi_ref[...], preferred_element_type=F32)
        h = jnp.square(jnp.maximum(h, 0.0)).astype(BF16)
        acc_ref[...] += jnp.dot(h, wo_ref[...], preferred_element_type=F32)

        @pl.when(f == nf - 1)
        def _():
            o_ref[...] = x_ref[...] + acc_ref[...]

    return pl.pallas_call(
        body,
        grid=(t_rows // tm, nf),
        in_specs=[
            pl.BlockSpec((tm, d), lambda i, f: (i, 0)),
            pl.BlockSpec((1, d), lambda i, f: (0, 0)),
            pl.BlockSpec((d, tf), lambda i, f: (0, f)),
            pl.BlockSpec((tf, d), lambda i, f: (f, 0)),
        ],
        out_specs=pl.BlockSpec((tm, d), lambda i, f: (i, 0)),
        out_shape=jax.ShapeDtypeStruct((t_rows, d), F32),
        scratch_shapes=[pltpu.VMEM((tm, d), BF16), pltpu.VMEM((tm, d), F32)],
        compiler_params=_params(2),
    )(x, gain, w_in, w_out)


def _ple_residual(x, p, gate_g, w_gate, w_ple, ple_g, *, tm=256):
    t_rows, d = x.shape
    pd = p.shape[1]

    def body(x_ref, p_ref, gg_ref, wg_ref, wp_ref, pg_ref, o_ref, xn_ref):
        _rms_to_bf16(x_ref, gg_ref, xn_ref)
        e = jnp.dot(p_ref[...].astype(BF16), wp_ref[...], preferred_element_type=F32)
        ms = jnp.mean(e * e, axis=-1, keepdims=True)
        e = e * lax.rsqrt(ms + EPS) * pg_ref[...]
        z = jnp.dot(xn_ref[...], wg_ref[...], preferred_element_type=F32)
        o_ref[...] = x_ref[...] + jax.nn.sigmoid(z) * e

    const = lambda i: (0, 0)
    return pl.pallas_call(
        body,
        grid=(t_rows // tm,),
        in_specs=[
            pl.BlockSpec((tm, d), lambda i: (i, 0)),
            pl.BlockSpec((tm, pd), lambda i: (i, 0)),
            pl.BlockSpec((1, d), const),
            pl.BlockSpec((d, d), const),
            pl.BlockSpec((pd, d), const),
            pl.BlockSpec((1, d), const),
        ],
        out_specs=pl.BlockSpec((tm, d), lambda i: (i, 0)),
        out_shape=jax.ShapeDtypeStruct((t_rows, d), F32),
        scratch_shapes=[pltpu.VMEM((tm, d), BF16)],
        compiler_params=_params(1),
    )(x, p, gate_g, w_gate, w_ple, ple_g)


def _rope_cos_sin(pos, dim, theta):
    inv_freq = theta ** (-jnp.arange(0, dim, 2, dtype=F32) / dim)
    ang = pos.astype(F32)[:, None] * inv_freq[None, :]
    return jnp.cos(ang), jnp.sin(ang)


def _pair_perm(first, second, rest):
    fill = LANES // 2 - len(first)
    perm = list(first) + list(rest[:fill]) + list(second) + list(rest[fill:])
    assert sorted(perm) == list(range(LANES))
    return np.asarray(perm)


A_PERM = _pair_perm(list(range(0, 32)) + list(range(64, 96)),
                    list(range(32, 64)) + list(range(96, 128)), [])
B_PERM = _pair_perm(range(0, 32), range(32, 64), list(range(64, 128)))
C_PERM = _pair_perm(range(0, 16), range(16, 32), list(range(32, 128)))


def _permute_heads(w_cols, n_heads, perm):
    lead = w_cols.shape[:-1]
    n_cols = n_heads * LANES
    heads = w_cols[..., :n_cols].reshape(lead + (n_heads, LANES))[..., perm]
    return jnp.concatenate([heads.reshape(lead + (n_cols,)), w_cols[..., n_cols:]], axis=-1)


def _axial_tables(seq):
    t = jnp.arange(seq)
    half = A_HEAD_DIM // 2
    cr, sr = _rope_cos_sin(t // GRID_W, half, A_ROPE_THETA)
    cc, sc = _rope_cos_sin(t % GRID_W, half, A_ROPE_THETA)
    return (jnp.concatenate([cr, cc, cr, cc], axis=-1),
            jnp.concatenate([-sr, -sc, sr, sc], axis=-1))


def _partial_tables(seq, rope_dim, theta):
    c, s = _rope_cos_sin(jnp.arange(seq), rope_dim, theta)
    fill = LANES // 2 - rope_dim // 2
    one, zero = jnp.ones((seq, fill), F32), jnp.zeros((seq, fill), F32)
    return (jnp.concatenate([c, one, c, one], axis=-1),
            jnp.concatenate([-s, zero, s, zero], axis=-1))


def _row(v):
    return v.reshape(1, -1).astype(F32)


def _prep_weights(w):
    out = {}
    out["w_mlp_in"] = w["w_mlp_in"].astype(BF16)
    out["w_mlp_out"] = w["w_mlp_out"].astype(BF16)
    out["w_ple"] = w["w_ple"].astype(BF16)
    out["w_ple_gate"] = w["w_ple_gate"].astype(BF16)
    out["a_w_qkv"] = _permute_heads(w["a_w_qkv"], A_HEADS + A_KV_HEADS, A_PERM).astype(BF16)
    out["a_w_o"] = w["a_w_o"].astype(BF16)
    out["a_head_g"] = jnp.concatenate(
        [jnp.tile(w["a_q_norm_g"][:, A_PERM] * (A_HEAD_DIM ** -0.5 * LOG2E), (1, A_HEADS)),
         jnp.tile(w["a_k_norm_g"][:, A_PERM], (1, A_KV_HEADS)),
         jnp.ones((w["a_q_norm_g"].shape[0], A_KV_HEADS * A_HEAD_DIM), F32)], axis=-1)

    n_b = w["b_w_dqkv"].shape[0]
    lat_w = B_Q_RANK + B_KV_RANK + B_ROPE_DIM
    n_lat_groups = B_LAT_PAD // LANES

    def permute_last_group(a, n_groups):
        lead = a.shape[:-1]
        grouped = a.reshape(lead + (n_groups, LANES))
        last = grouped[..., n_groups - 1:, :][..., B_PERM]
        return jnp.concatenate([grouped[..., :n_groups - 1, :], last], axis=-2).reshape(a.shape)

    dqkv = jnp.pad(w["b_w_dqkv"], ((0, 0), (0, 0), (0, B_LAT_PAD - lat_w)))
    out["b_w_dqkv"] = permute_last_group(dqkv, n_lat_groups).astype(BF16)
    uq = w["b_w_uq"].reshape(n_b, B_Q_RANK, B_HEADS, B_QK_DIM)
    uq = jnp.pad(uq, ((0, 0), (0, 0), (0, 0), (0, B_HEAD_PAD - B_QK_DIM)))
    uq = permute_last_group(uq, B_HEAD_PAD // LANES)
    out["b_w_uq"] = uq.reshape(n_b, B_Q_RANK, B_HEADS * B_HEAD_PAD).astype(BF16)
    ukv = w["b_w_ukv"].reshape(n_b, B_KV_RANK, B_HEADS, B_NOPE_DIM + B_V_DIM)
    out["b_w_uk"] = ukv[..., :B_NOPE_DIM].reshape(n_b, B_KV_RANK, B_HEADS * B_NOPE_DIM).astype(BF16)
    out["b_w_uv"] = ukv[..., B_NOPE_DIM:].reshape(n_b, B_KV_RANK, B_HEADS * B_V_DIM).astype(BF16)
    pad_g = ((0, 0), (0, B_HEAD_PAD - B_QK_DIM))
    out["b_q_g"] = permute_last_group(
        jnp.pad(w["b_q_norm_g"] * (B_QK_DIM ** -0.5 * LOG2E), pad_g), B_HEAD_PAD // LANES)
    out["b_k_g"] = permute_last_group(jnp.pad(w["b_k_norm_g"], pad_g), B_HEAD_PAD // LANES)
    out["b_w_o"] = w["b_w_o"].astype(BF16)

    n_ch = C_GROUPS * C_HEADS
    hw = C_HEADS * C_HEAD_DIM

    def group_major(a):
        lead = a.shape[:-1]
        a = a.reshape(lead + (3, C_GROUPS, hw))
        return jnp.swapaxes(a, -3, -2).reshape(lead + (3 * C_GROUPS * hw,))

    out["c_w_qkv"] = group_major(_permute_heads(w["c_w_qkv"], 2 * n_ch, C_PERM)).astype(BF16)
    out["c_w_o"] = w["c_w_o"].astype(BF16)
    out["c_head_g"] = group_major(jnp.concatenate(
        [jnp.tile(w["c_q_norm_g"][:, C_PERM] * (C_HEAD_DIM ** -0.5 * LOG2E), (1, n_ch)),
         jnp.tile(w["c_k_norm_g"][:, C_PERM], (1, n_ch)),
         jnp.ones((w["c_q_norm_g"].shape[0], n_ch * C_HEAD_DIM), F32)], axis=-1))
    return out


def _mixer_a(x, norm_g, w, pw, j, batch, seq):
    cos, sin = _axial_tables(seq)
    w_qkv = pw["a_w_qkv"][j]
    qk, vt = _qkv_proj(x, w_qkv, norm_g, _row(pw["a_head_g"][j]), cos, sin, seq=seq,
                       n_normed_cols=(A_HEADS + A_KV_HEADS) * A_HEAD_DIM, tm=512,
                       tn=w_qkv.shape[1], n_transposed_cols=A_KV_HEADS * A_HEAD_DIM)
    group = A_HEADS // A_KV_HEADS
    o = _dense_attention(qk, qk, vt, batch=batch, seq=seq, n_groups=A_KV_HEADS,
                         heads=group, kv_heads=1, dq=A_HEAD_DIM,
                         q_cb0=0, k_cb0=A_HEADS)
    return _out_proj_residual(o, pw["a_w_o"][j], x)


def _mixer_b(x, norm_g, w, pw, j, batch, seq):
    cos, sin = _partial_tables(seq, B_ROPE_DIM, B_ROPE_THETA)
    tm = 512
    n_pos_blocks = seq // tm
    lat = _fused_matmul(x, pw["b_w_dqkv"][j], tm=tm, tn=B_LAT_PAD, chunk=B_LAT_PAD // 3,
                        gain=norm_g, out_dtype=F32)

    def pos_spec():
        return pl.BlockSpec((tm, LANES), lambda i, jj: (i % n_pos_blocks, 0))

    head_g_spec = pl.BlockSpec((1, B_HEAD_PAD), lambda i, jj: (0, 0))
    q = _fused_matmul(lat, pw["b_w_uq"][j], tm=tm, tn=B_HEADS * B_HEAD_PAD, chunk=512,
                      gain=_row(w["b_cq_norm_g"][j]), x_kblock=0,
                      extras=[(_row(pw["b_q_g"][j]), head_g_spec), (cos, pos_spec()), (sin, pos_spec())],
                      epilogue=_mla_q_epilogue)
    ckv_g = _row(w["b_ckv_norm_g"][j])
    k_rope_block = (B_Q_RANK + B_KV_RANK) // LANES
    k = _fused_matmul(lat, pw["b_w_uk"][j], tm=tm, tn=B_HEADS * B_NOPE_DIM, chunk=512,
                      gain=ckv_g, x_kblock=1,
                      extras=[(lat, pl.BlockSpec((tm, LANES), lambda i, jj: (i, k_rope_block))),
                              (_row(pw["b_k_g"][j]), head_g_spec), (cos, pos_spec()), (sin, pos_spec())],
                      epilogue=_mla_k_epilogue, out_tile=B_HEADS * B_HEAD_PAD)
    vt = _fused_matmul(lat, pw["b_w_uv"][j], tm=tm, tn=B_HEADS * B_V_DIM, chunk=512,
                       gain=ckv_g, x_kblock=1, n_transposed_cols=B_HEADS * B_V_DIM)
    heads = 4
    o = _dense_attention(q, k, vt, batch=batch, seq=seq, n_groups=B_HEADS // heads,
                         heads=heads, kv_heads=heads, dq=B_HEAD_PAD,
                         q_cb0=0, k_cb0=0)
    return _out_proj_residual(o, pw["b_w_o"][j], x)


def _mixer_c(x, norm_g, w, pw, j, batch, seq):
    cos, sin = _partial_tables(seq, C_ROPE_DIM, C_ROPE_THETA)
    n_ch = C_GROUPS * C_HEADS
    by_group = _dilated_qkv_proj(x, pw["c_w_qkv"][j], norm_g, _row(pw["c_head_g"][j]), cos, sin,
                                 batch=batch, seq=seq)
    os, lses = [], []
    for xg, (_, dil) in zip(by_group, C_PAIRS):
        o, lse = _banded_attention(xg.reshape(batch * seq, xg.shape[-1]), seq_len=seq // dil)
        os.append(o)
        lses.append(lse)
    return _merge_out_proj_residual(os, lses, pw["c_w_o"][j], x, batch=batch, seq=seq)


def _layer_stack(x, p, w, pw):
    batch, seq, d = x.shape
    x = x.reshape(batch * seq, d)
    p = p.reshape(DEPTH, batch * seq, PLE_DIM)
    for i in range(DEPTH):
        kind, j = i % N_MIXERS, i // N_MIXERS
        mixer = (_mixer_a, _mixer_b, _mixer_c)[kind]
        x = mixer(x, _row(w["norm_mix_g"][i]), w, pw, j, batch, seq)
        x = _mlp_residual(x, _row(w["norm_mlp_g"][i]), pw["w_mlp_in"][i], pw["w_mlp_out"][i])
        x = _ple_residual(x, p[i], _row(w["ple_gate_norm_g"][i]), pw["w_ple_gate"][i],
                          pw["w_ple"][i], _row(w["ple_norm_g"][i]))
    return x.reshape(batch, seq, d)


def kernel(x_prompt, x_sample, p_prompt, p_sample, norm_mix_g, norm_mlp_g, w_mlp_in, w_mlp_out, w_ple, ple_norm_g, ple_gate_norm_g, w_ple_gate, a_w_qkv, a_q_norm_g, a_k_norm_g, a_w_o, b_w_dqkv, b_cq_norm_g, b_ckv_norm_g, b_w_uq, b_w_ukv, b_q_norm_g, b_k_norm_g, b_w_o, c_w_qkv, c_q_norm_g, c_k_norm_g, c_w_o):
    w = dict(norm_mix_g=norm_mix_g, norm_mlp_g=norm_mlp_g, w_mlp_in=w_mlp_in, w_mlp_out=w_mlp_out,
             w_ple=w_ple, ple_norm_g=ple_norm_g, ple_gate_norm_g=ple_gate_norm_g,
             w_ple_gate=w_ple_gate, a_w_qkv=a_w_qkv, a_q_norm_g=a_q_norm_g, a_k_norm_g=a_k_norm_g,
             a_w_o=a_w_o, b_w_dqkv=b_w_dqkv, b_cq_norm_g=b_cq_norm_g, b_ckv_norm_g=b_ckv_norm_g,
             b_w_uq=b_w_uq, b_w_ukv=b_w_ukv, b_q_norm_g=b_q_norm_g, b_k_norm_g=b_k_norm_g,
             b_w_o=b_w_o, c_w_qkv=c_w_qkv, c_q_norm_g=c_q_norm_g, c_k_norm_g=c_k_norm_g, c_w_o=c_w_o)
    pw = _prep_weights(w)
    return (_layer_stack(x_prompt, p_prompt, w, pw), _layer_stack(x_sample, p_sample, w, pw))
```

```python
import functools
import math

import numpy as np
import jax
import jax.numpy as jnp
from jax import lax
from jax.experimental import pallas as pl
from jax.experimental.pallas import tpu as pltpu

F32 = jnp.float32
BF16 = jnp.bfloat16

D_MODEL = 2048
DEPTH = 4
N_MIXERS = 3
PLE_DIM = 256
GRID_W = 64
EPS = 1e-6
D_FF = 4 * D_MODEL

A_HEADS = 16
A_KV_HEADS = 4
A_HEAD_DIM = 128
A_ROPE_THETA = 10000.0

B_HEADS = 16
B_Q_RANK = 512
B_KV_RANK = 512
B_NOPE_DIM = 128
B_ROPE_DIM = 64
B_V_DIM = 128
B_QK_DIM = B_NOPE_DIM + B_ROPE_DIM
B_ROPE_THETA = 10000.0
B_HEAD_PAD = 256
B_LAT_PAD = 1152

C_PAIRS = ((128, 1), (512, 4), (2048, 16))
C_GROUPS = 3
C_HEADS = 8
C_HEAD_DIM = 128
C_ROPE_DIM = 32
C_ROPE_THETA = 500000.0

LANES = 128
DEN_ROWS = 16
LOG2E = math.log2(math.e)
MASK_BIAS = -1e30
VMEM_LIMIT = 56 * 1024 * 1024


def _params(n_axes):
    return pltpu.CompilerParams(dimension_semantics=("arbitrary",) * n_axes,
                                vmem_limit_bytes=VMEM_LIMIT)


def _rms_to_bf16(x_ref, g_ref, xn_ref, rows=128):
    tm = x_ref.shape[0]

    def chunk(c, carry):
        r0 = pl.multiple_of(c * rows, rows)
        xf = x_ref[pl.ds(r0, rows), :]
        ms = jnp.mean(xf * xf, axis=-1, keepdims=True)
        xn_ref[pl.ds(r0, rows), :] = (xf * lax.rsqrt(ms + EPS) * g_ref[...]).astype(BF16)
        return carry

    lax.fori_loop(0, tm // rows, chunk, 0)


def _rope(y, cos, sin):
    return y * cos + pltpu.roll(y, LANES // 2, 1) * sin


def _fused_matmul(x, w, *, tm, tn, chunk, gain=None, x_kblock=0, extras=(), epilogue=None,
                  n_epilogue_cols=None, out_tile=None, out_dtype=BF16, n_transposed_cols=0):
    t_rows = x.shape[0]
    k_dim, n_dim = w.shape
    nj = n_dim // tn
    ow = out_tile or tn
    norm = gain is not None
    n_extra = len(extras)
    n_epi = 0 if epilogue is None else (n_dim if n_epilogue_cols is None else n_epilogue_cols)
    assert nj == 1 or n_epi % tn == 0
    assert n_epi == n_dim or ow == tn
    n_t = n_transposed_cols
    n_main = n_dim - n_t
    assert n_t == 0 or (nj == 1 and n_t % chunk == 0 and n_epi <= n_main and ow == tn)

    def body(*refs):
        x_ref, w_ref = refs[0], refs[1]
        pos = 2
        if norm:
            g_ref = refs[pos]
            pos += 1
        extra_refs = refs[pos:pos + n_extra]
        pos += n_extra
        o_ref = t_ref = None
        if n_main:
            o_ref = refs[pos]
            pos += 1
        if n_t:
            t_ref = refs[pos]
            pos += 1
        j = pl.program_id(1)
        if norm:
            lhs_ref = refs[pos]

            @pl.when(j == 0)
            def _():
                _rms_to_bf16(x_ref, g_ref, lhs_ref)
        else:
            lhs_ref = x_ref

        def run(with_epilogue):
            for c0 in range(0, tn, chunk):
                acc = jnp.dot(lhs_ref[...], w_ref[:, c0:c0 + chunk], preferred_element_type=F32)
                if c0 >= n_main:
                    t_ref[c0 - n_main:c0 - n_main + chunk, :] = acc.T.astype(BF16)
                elif with_epilogue(c0):
                    epilogue(acc, c0, extra_refs, o_ref)
                else:
                    o_ref[:, c0:c0 + chunk] = acc.astype(o_ref.dtype)

        if nj == 1:
            run(lambda c0: c0 < n_epi)
        elif n_epi in (0, n_dim):
            run(lambda c0: n_epi > 0)
        else:
            @pl.when(j < n_epi // tn)
            def _():
                run(lambda c0: True)

            @pl.when(j >= n_epi // tn)
            def _():
                run(lambda c0: False)

    in_specs = [pl.BlockSpec((tm, k_dim), lambda i, j: (i, x_kblock)),
                pl.BlockSpec((k_dim, tn), lambda i, j: (0, j))]
    args = [x, w]
    if norm:
        in_specs.append(pl.BlockSpec((1, k_dim), lambda i, j: (0, 0)))
        args.append(gain)
    for arr, spec in extras:
        in_specs.append(spec)
        args.append(arr)
    scratch = [pltpu.VMEM((tm, k_dim), BF16)] if norm else []
    out_specs, out_shape = [], []
    if n_main:
        main_w = ow if n_t == 0 else n_main
        out_specs.append(pl.BlockSpec((tm, main_w), lambda i, j: (i, j)))
        out_shape.append(jax.ShapeDtypeStruct((t_rows, nj * main_w), out_dtype))
    if n_t:
        out_specs.append(pl.BlockSpec((n_t, tm), lambda i, j: (0, i)))
        out_shape.append(jax.ShapeDtypeStruct((n_t, t_rows), BF16))
    if len(out_specs) == 1:
        out_specs, out_shape = out_specs[0], out_shape[0]
    return pl.pallas_call(
        body,
        grid=(t_rows // tm, nj),
        in_specs=in_specs,
        out_specs=out_specs,
        out_shape=out_shape,
        scratch_shapes=scratch,
        compiler_params=_params(2),
    )(*args)


def _residual_epilogue(acc, c0, extra_refs, o_ref):
    (res_ref,) = extra_refs
    cols = slice(c0, c0 + acc.shape[1])
    o_ref[:, cols] = res_ref[:, cols] + acc


def _out_proj_residual(o, w_o, x, *, tm=512):
    n_dim = w_o.shape[1]
    extras = [(x, pl.BlockSpec((tm, n_dim), lambda i, j: (i, 0)))]
    return _fused_matmul(o, w_o, tm=tm, tn=n_dim, chunk=512, extras=extras,
                         epilogue=_residual_epilogue, out_dtype=F32)


def _head_norm_rope(c, gain, cos, sin):
    ms = jnp.mean(c * c, axis=-1, keepdims=True)
    return _rope(c * lax.rsqrt(ms + EPS) * gain, cos, sin)


def _head_norm_rope_epilogue(acc, c0, extra_refs, o_ref):
    g_ref, cos_ref, sin_ref = extra_refs
    for h in range(acc.shape[1] // LANES):
        cols = slice(c0 + h * LANES, c0 + (h + 1) * LANES)
        y = _head_norm_rope(acc[:, h * LANES:(h + 1) * LANES], g_ref[:, cols],
                            cos_ref[...], sin_ref[...])
        o_ref[:, cols] = y.astype(o_ref.dtype)


def _qkv_proj(x, w, norm_g, head_g, cos, sin, *, seq, n_normed_cols, tm, tn, n_transposed_cols=0):
    n_pos_blocks = seq // tm
    extras = [
        (head_g, pl.BlockSpec((1, tn), lambda i, j: (0, j))),
        (cos, pl.BlockSpec((tm, LANES), lambda i, j: (i % n_pos_blocks, 0))),
        (sin, pl.BlockSpec((tm, LANES), lambda i, j: (i % n_pos_blocks, 0))),
    ]
    return _fused_matmul(x, w, tm=tm, tn=tn, chunk=512, gain=norm_g, extras=extras,
                         epilogue=_head_norm_rope_epilogue, n_epilogue_cols=n_normed_cols,
                         n_transposed_cols=n_transposed_cols)


def _mla_q_epilogue(acc, c0, extra_refs, o_ref):
    g_ref, cos_ref, sin_ref = extra_refs
    for h in range(acc.shape[1] // B_HEAD_PAD):
        c_lo = acc[:, h * B_HEAD_PAD:h * B_HEAD_PAD + LANES]
        c_hi = acc[:, h * B_HEAD_PAD + LANES:(h + 1) * B_HEAD_PAD]
        ss = (jnp.sum(c_lo * c_lo, axis=-1, keepdims=True)
              + jnp.sum(c_hi * c_hi, axis=-1, keepdims=True)) * (1.0 / B_QK_DIM)
        r = lax.rsqrt(ss + EPS)
        base = c0 + h * B_HEAD_PAD
        o_ref[:, base:base + LANES] = (c_lo * r * g_ref[:, :LANES]).astype(o_ref.dtype)
        y_hi = c_hi * r * g_ref[:, LANES:]
        o_ref[:, base + LANES:base + B_HEAD_PAD] = _rope(
            y_hi, cos_ref[...], sin_ref[...]).astype(o_ref.dtype)


def _mla_k_epilogue(acc, c0, extra_refs, o_ref):
    kr_ref, g_ref, cos_ref, sin_ref = extra_refs
    kr = kr_ref[...]
    kr_ss = jnp.sum(kr * kr, axis=-1, keepdims=True)
    for h in range(acc.shape[1] // LANES):
        c = acc[:, h * LANES:(h + 1) * LANES]
        ss = (jnp.sum(c * c, axis=-1, keepdims=True) + kr_ss) * (1.0 / B_QK_DIM)
        r = lax.rsqrt(ss + EPS)
        base = (c0 // LANES + h) * B_HEAD_PAD
        o_ref[:, base:base + LANES] = (c * r * g_ref[:, :LANES]).astype(o_ref.dtype)
        y_hi = kr * r * g_ref[:, LANES:]
        o_ref[:, base + LANES:base + B_HEAD_PAD] = _rope(
            y_hi, cos_ref[...], sin_ref[...]).astype(o_ref.dtype)


def _dense_attention(q_arr, k_arr, vt_arr, *, batch, seq, n_groups, heads, kv_heads, dq,
                     q_cb0, k_cb0, tq=1024, tk=1024):
    nq, nk = seq // tq, seq // tk
    dv = LANES

    def body(q_ref, k_ref, vt_ref, o_ref, qt_ref, m_ref, acc_ref, s_ref, mc_ref):
        ki = pl.program_id(3)

        def score_phase(slot):
            for g in range(heads):
                kv = g * kv_heads // heads
                s = jnp.dot(k_ref[:, kv * dq:(kv + 1) * dq], qt_ref[g],
                            preferred_element_type=F32)
                s_ref[slot, g] = s
                mc_ref[slot, g] = jnp.max(s, axis=0, keepdims=True)

        def value_phase(slot):
            ones_rows = jnp.ones((DEN_ROWS, tk), BF16)
            for g in range(heads):
                kv = g * kv_heads // heads
                m_prev = m_ref[g]
                m_new = jnp.maximum(m_prev, mc_ref[slot, g])
                alpha = jnp.exp2(m_prev - m_new)
                p = jnp.exp2(s_ref[slot, g] - m_new).astype(BF16)
                vt_ext = jnp.concatenate([vt_ref[kv * dv:(kv + 1) * dv, :], ones_rows], axis=0)
                pv = jnp.dot(vt_ext, p, preferred_element_type=F32)
                acc_ref[g] = acc_ref[g] * alpha + pv
                m_ref[g] = m_new

        @pl.when(ki == 0)
        def _():
            for g in range(heads):
                qg = q_ref[:, g * dq:(g + 1) * dq].astype(F32)
                qt_ref[g] = qg.T.astype(BF16)
            m_ref[...] = jnp.full(m_ref.shape, -jnp.inf, F32)
            acc_ref[...] = jnp.zeros(acc_ref.shape, F32)
            score_phase(0)

        for parity in (0, 1):
            @pl.when((ki > 0) & (ki < nk) & (ki % 2 == parity))
            def _():
                score_phase(parity)
                value_phase(1 - parity)

        @pl.when(ki == nk)
        def _():
            value_phase((nk - 1) % 2)
            for g in range(heads):
                o = acc_ref[g, :dv, :] / acc_ref[g, dv:dv + 1, :]
                o_ref[:, g * dv:(g + 1) * dv] = o.T.astype(o_ref.dtype)

    t_rows = batch * seq
    return pl.pallas_call(
        body,
        grid=(batch, n_groups, nq, nk + 1),
        in_specs=[
            pl.BlockSpec((tq, heads * dq), lambda b, g, qi, ki: (b * nq + qi, q_cb0 + g)),
            pl.BlockSpec((tk, kv_heads * dq),
                         lambda b, g, qi, ki: (b * nk + jnp.minimum(ki, nk - 1), k_cb0 + g)),
            pl.BlockSpec((kv_heads * dv, tk),
                         lambda b, g, qi, ki: (g, b * nk + jnp.maximum(ki - 1, 0))),
        ],
        out_specs=pl.BlockSpec((tq, heads * dv), lambda b, g, qi, ki: (b * nq + qi, g)),
        out_shape=jax.ShapeDtypeStruct((t_rows, n_groups * heads * dv), BF16),
        scratch_shapes=[
            pltpu.VMEM((heads, dq, tq), BF16),
            pltpu.VMEM((heads, 1, tq), F32),
            pltpu.VMEM((heads, dv + DEN_ROWS, tq), F32),
            pltpu.VMEM((2, heads, tk, tq), F32),
            pltpu.VMEM((2, heads, 1, tq), F32),
        ],
        compiler_params=_params(4),
    )(q_arr, k_arr, vt_arr)


C_TL = 256
C_REACH = 64
assert all(window // (2 * dil) == C_REACH for window, dil in C_PAIRS)
assert tuple(dil for _, dil in C_PAIRS) == (1, 4, 16)


def _band_bias(tl):
    i = np.arange(tl)[None, :]
    jh = np.arange(C_REACH)[:, None]
    jc = np.arange(tl)[:, None]

    def as_bias(ok):
        return jnp.asarray(np.where(ok, 0.0, MASK_BIAS), F32)

    return as_bias(jh >= i), as_bias(np.abs(jc - i) <= C_REACH), as_bias(i >= tl - C_REACH + jh)


def _dilated_qkv_proj(x, w, norm_g, head_g, cos, sin, *, batch, seq, tm=1024, chunk=512):
    t_rows, k_dim = x.shape
    hw = C_HEADS * C_HEAD_DIM
    n_steps = 3 * C_GROUPS
    blocks_per_seq = seq // tm
    dils = [dil for _, dil in C_PAIRS]
    assert all(tm % dil == 0 and (tm // dil) % 16 == 0 for dil in dils)

    def body(x_ref, w_ref, g_ref, hg_ref, cos_ref, sin_ref, o0_ref, o1_ref, o2_ref, xn_ref, res_ref):
        j = pl.program_id(1)

        @pl.when(j == 0)
        def _():
            _rms_to_bf16(x_ref, g_ref, xn_ref)

        def matmul(with_epilogue):
            for c0 in range(0, hw, chunk):
                acc = jnp.dot(xn_ref[...], w_ref[:, c0:c0 + chunk], preferred_element_type=F32)
                for h in range(chunk // LANES):
                    c = acc[:, h * LANES:(h + 1) * LANES]
                    head = c0 // LANES + h
                    if with_epilogue:
                        c = _head_norm_rope(c, hg_ref[:, head * LANES:(head + 1) * LANES],
                                            cos_ref[...], sin_ref[...])
                    res_ref[head] = c

        @pl.when(j % 3 < 2)
        def _():
            matmul(True)

        @pl.when(j % 3 == 2)
        def _():
            matmul(False)

        for g, (o_ref, dil) in enumerate(zip((o0_ref, o1_ref, o2_ref), dils)):
            @pl.when(j // 3 == g)
            def _():
                for r in range(dil):
                    for head in range(C_HEADS):
                        o_ref[r, :, head * LANES:(head + 1) * LANES] = res_ref[
                            head, pl.ds(r, tm // dil, stride=dil), :].astype(BF16)

    def out_spec(g, dil):
        return pl.BlockSpec(
            (dil, tm // dil, hw),
            lambda i, j: (i // blocks_per_seq, i % blocks_per_seq, jnp.clip(j - 3 * g, 0, 2)))

    return pl.pallas_call(
        body,
        grid=(t_rows // tm, n_steps),
        in_specs=[
            pl.BlockSpec((tm, k_dim), lambda i, j: (i, 0)),
            pl.BlockSpec((k_dim, hw), lambda i, j: (0, j)),
            pl.BlockSpec((1, k_dim), lambda i, j: (0, 0)),
            pl.BlockSpec((1, hw), lambda i, j: (0, j)),
            pl.BlockSpec((tm, LANES), lambda i, j: (i % blocks_per_seq, 0)),
            pl.BlockSpec((tm, LANES), lambda i, j: (i % blocks_per_seq, 0)),
        ],
        out_specs=[out_spec(g, dil) for g, dil in enumerate(dils)],
        out_shape=[jax.ShapeDtypeStruct((batch * dil, seq // dil, 3 * hw), BF16) for dil in dils],
        scratch_shapes=[pltpu.VMEM((tm, k_dim), BF16), pltpu.VMEM((C_HEADS, tm, LANES), F32)],
        compiler_params=_params(2),
    )(x, w, norm_g, head_g, cos, sin)


def _banded_attention(x, *, seq_len, tl=C_TL):
    hw = C_HEADS * C_HEAD_DIM
    dh = C_HEAD_DIM
    n_rows = x.shape[0]
    halo_per_block = tl // C_REACH
    n_halo_blocks = n_rows // C_REACH
    nb_seq = seq_len // tl
    assert seq_len % tl == 0
    bias_prev, bias_cur, bias_next = _band_bias(tl)
    nt_dims = (((1,), (1,)), ((), ()))
    tn_dims = (((0,), (0,)), ((), ()))

    def body(q_ref, kp_ref, kc_ref, kn_ref, vp_ref, vc_ref, vn_ref, bp_ref, bc_ref, bn_ref,
             o_ref, lse_ref):
        pos = pl.program_id(0) % nb_seq
        b_prev = bp_ref[...] + jnp.where(pos == 0, MASK_BIAS, 0.0)
        b_next = bn_ref[...] + jnp.where(pos == nb_seq - 1, MASK_BIAS, 0.0)
        b_cur = bc_ref[...]
        lse_rows = []
        for h in range(C_HEADS):
            sl = slice(h * dh, (h + 1) * dh)
            q = q_ref[:, sl]
            s_p = lax.dot_general(kp_ref[:, sl], q, nt_dims, preferred_element_type=F32) + b_prev
            s_c = lax.dot_general(kc_ref[:, sl], q, nt_dims, preferred_element_type=F32) + b_cur
            s_n = lax.dot_general(kn_ref[:, sl], q, nt_dims, preferred_element_type=F32) + b_next
            m = jnp.maximum(jnp.max(s_c, axis=0, keepdims=True),
                            jnp.maximum(jnp.max(s_p, axis=0, keepdims=True),
                                        jnp.max(s_n, axis=0, keepdims=True)))
            p_p = jnp.exp2(s_p - m)
            p_c = jnp.exp2(s_c - m)
            p_n = jnp.exp2(s_n - m)
            l = (jnp.sum(p_c, axis=0, keepdims=True) + jnp.sum(p_p, axis=0, keepdims=True)
                 + jnp.sum(p_n, axis=0, keepdims=True))
            ot = (lax.dot_general(vc_ref[:, sl], p_c.astype(BF16), tn_dims, preferred_element_type=F32)
                  + lax.dot_general(vp_ref[:, sl], p_p.astype(BF16), tn_dims, preferred_element_type=F32)
                  + lax.dot_general(vn_ref[:, sl], p_n.astype(BF16), tn_dims, preferred_element_type=F32))
            o_ref[:, sl] = (ot / l).T
            lse_rows.append((m + jnp.log2(l)) * (1.0 / LOG2E))
        lse_t = jnp.concatenate(lse_rows + [jnp.zeros((LANES - C_HEADS, tl), F32)], axis=0)
        lse_ref[...] = lse_t.T

    def prev_halo(blk):
        return jnp.maximum(blk * halo_per_block - 1, 0)

    def next_halo(blk):
        return jnp.minimum((blk + 1) * halo_per_block, n_halo_blocks - 1)

    const = lambda blk: (0, 0)
    return pl.pallas_call(
        body,
        grid=(n_rows // tl,),
        in_specs=[
            pl.BlockSpec((tl, hw), lambda blk: (blk, 0)),
            pl.BlockSpec((C_REACH, hw), lambda blk: (prev_halo(blk), 1)),
            pl.BlockSpec((tl, hw), lambda blk: (blk, 1)),
            pl.BlockSpec((C_REACH, hw), lambda blk: (next_halo(blk), 1)),
            pl.BlockSpec((C_REACH, hw), lambda blk: (prev_halo(blk), 2)),
            pl.BlockSpec((tl, hw), lambda blk: (blk, 2)),
            pl.BlockSpec((C_REACH, hw), lambda blk: (next_halo(blk), 2)),
            pl.BlockSpec((C_REACH, tl), const),
            pl.BlockSpec((tl, tl), const),
            pl.BlockSpec((C_REACH, tl), const),
        ],
        out_specs=[pl.BlockSpec((tl, hw), lambda blk: (blk, 0)),
                   pl.BlockSpec((tl, LANES), lambda blk: (blk, 0))],
        out_shape=[jax.ShapeDtypeStruct((n_rows, hw), F32),
                   jax.ShapeDtypeStruct((n_rows, LANES), F32)],
        compiler_params=_params(1),
    )(x, x, x, x, x, x, x, bias_prev, bias_cur, bias_next)


def _merge_out_proj_residual(os, lses, w_o, x, *, batch, seq, tm=512, tn=512):
    t_rows = x.shape[0]
    hw, n_dim = w_o.shape
    dh = C_HEAD_DIM
    dils = [dil for _, dil in C_PAIRS]
    blocks_per_seq = seq // tm
    assert all(tm % dil == 0 and (tm // dil) % 8 == 0 for dil in dils)

    def body(o0_ref, o1_ref, o2_ref, l0_ref, l1_ref, l2_ref, w_ref, res_ref, out_ref,
             ot_ref, lt_ref, mg_ref):
        @pl.when(pl.program_id(1) == 0)
        def _():
            for g, (o_ref, l_ref, dil) in enumerate(zip((o0_ref, o1_ref, o2_ref),
                                                        (l0_ref, l1_ref, l2_ref), dils)):
                for r in range(dil):
                    rows = pl.ds(r, tm // dil, stride=dil)
                    for h in range(C_HEADS):
                        ot_ref[g, h, rows, :] = o_ref[r, :, h * dh:(h + 1) * dh]
                    lt_ref[g, rows, :] = l_ref[r]
            lse = [lt_ref[g] for g in range(C_GROUPS)]
            mx = jnp.maximum(lse[0], jnp.maximum(lse[1], lse[2]))
            e = [jnp.exp(v - mx) for v in lse]
            den = e[0] + e[1] + e[2]
            wts = [v / den for v in e]
            for h in range(C_HEADS):
                sl = slice(h * dh, (h + 1) * dh)
                mg = (wts[0][:, h:h + 1] * ot_ref[0, h] + wts[1][:, h:h + 1] * ot_ref[1, h]
                      + wts[2][:, h:h + 1] * ot_ref[2, h])
                mg_ref[:, sl] = mg.astype(BF16)

        out_ref[...] = res_ref[...] + jnp.dot(mg_ref[...], w_ref[...], preferred_element_type=F32)

    def group_spec(dil, width):
        return pl.BlockSpec((dil, tm // dil, width),
                            lambda i, j: (i // blocks_per_seq, i % blocks_per_seq, 0))

    args = ([o.reshape(batch * dil, seq // dil, hw) for o, dil in zip(os, dils)]
            + [l.reshape(batch * dil, seq // dil, LANES) for l, dil in zip(lses, dils)])
    return pl.pallas_call(
        body,
        grid=(t_rows // tm, n_dim // tn),
        in_specs=([group_spec(dil, hw) for dil in dils] + [group_spec(dil, LANES) for dil in dils]
                  + [pl.BlockSpec((hw, tn), lambda i, j: (0, j)),
                     pl.BlockSpec((tm, tn), lambda i, j: (i, j))]),
        out_specs=pl.BlockSpec((tm, tn), lambda i, j: (i, j)),
        out_shape=jax.ShapeDtypeStruct((t_rows, n_dim), F32),
        scratch_shapes=[pltpu.VMEM((C_GROUPS, C_HEADS, tm, dh), F32),
                        pltpu.VMEM((C_GROUPS, tm, LANES), F32),
                        pltpu.VMEM((tm, hw), BF16)],
        compiler_params=_params(2),
    )(*args, w_o, x)


def _mlp_residual(x, gain, w_in, w_out, *, tm=1024, tf=1024):
    t_rows, d = x.shape
    nf = w_in.shape[1] // tf

    def body(x_ref, g_ref, wi_ref, wo_ref, o_ref, xn_ref):
        @pl.when(pl.program_id(1) == 0)
        def _():
            _rms_to_bf16(x_ref, g_ref, xn_ref)
            o_ref[...] = x_ref[...]

        h = jnp.dot(xn_ref[...], wi_ref[...], preferred_element_type=F32)
        h = jnp.square(jnp.maximum(h, 0.0)).astype(BF16)
        o_ref[...] += jnp.dot(h, wo_ref[...], preferred_element_type=F32)

    return pl.pallas_call(
        body,
        grid=(t_rows // tm, nf),
        in_specs=[
            pl.BlockSpec((tm, d), lambda i, f: (i, 0), pipeline_mode=pl.Buffered(1)),
            pl.BlockSpec((1, d), lambda i, f: (0, 0)),
            pl.BlockSpec((d, tf), lambda i, f: (0, f)),
            pl.BlockSpec((tf, d), lambda i, f: (f, 0)),
        ],
        out_specs=pl.BlockSpec((tm, d), lambda i, f: (i, 0)),
        out_shape=jax.ShapeDtypeStruct((t_rows, d), F32),
        scratch_shapes=[pltpu.VMEM((tm, d), BF16)],
        compiler_params=_params(2),
    )(x, gain, w_in, w_out)


def _ple_residual(x, p, gate_g, w_gate, w_ple, ple_g, *, tm=512):
    t_rows, d = x.shape
    pd = p.shape[1]

    def body(x_ref, p_ref, gg_ref, wg_ref, wp_ref, pg_ref, o_ref, xn_ref):
        _rms_to_bf16(x_ref, gg_ref, xn_ref)
        e = jnp.dot(p_ref[...].astype(BF16), wp_ref[...], preferred_element_type=F32)
        ms = jnp.mean(e * e, axis=-1, keepdims=True)
        e = e * lax.rsqrt(ms + EPS) * pg_ref[...]
        z = jnp.dot(xn_ref[...], wg_ref[...], preferred_element_type=F32)
        o_ref[...] = x_ref[...] + jax.nn.sigmoid(z) * e

    const = lambda i: (0, 0)
    return pl.pallas_call(
        body,
        grid=(t_rows // tm,),
        in_specs=[
            pl.BlockSpec((tm, d), lambda i: (i, 0)),
            pl.BlockSpec((tm, pd), lambda i: (i, 0)),
            pl.BlockSpec((1, d), const),
            pl.BlockSpec((d, d), const, pipeline_mode=pl.Buffered(1)),
            pl.BlockSpec((pd, d), const, pipeline_mode=pl.Buffered(1)),
            pl.BlockSpec((1, d), const),
        ],
        out_specs=pl.BlockSpec((tm, d), lambda i: (i, 0)),
        out_shape=jax.ShapeDtypeStruct((t_rows, d), F32),
        scratch_shapes=[pltpu.VMEM((tm, d), BF16)],
        compiler_params=_params(1),
    )(x, p, gate_g, w_gate, w_ple, ple_g)


def _rope_cos_sin(pos, dim, theta):
    inv_freq = theta ** (-jnp.arange(0, dim, 2, dtype=F32) / dim)
    ang = pos.astype(F32)[:, None] * inv_freq[None, :]
    return jnp.cos(ang), jnp.sin(ang)


def _pair_perm(first, second, rest):
    fill = LANES // 2 - len(first)
    perm = list(first) + list(rest[:fill]) + list(second) + list(rest[fill:])
    assert sorted(perm) == list(range(LANES))
    return np.asarray(perm)


A_PERM = _pair_perm(list(range(0, 32)) + list(range(64, 96)),
                    list(range(32, 64)) + list(range(96, 128)), [])
B_PERM = _pair_perm(range(0, 32), range(32, 64), list(range(64, 128)))
C_PERM = _pair_perm(range(0, 16), range(16, 32), list(range(32, 128)))


def _permute_heads(w_cols, n_heads, perm):
    lead = w_cols.shape[:-1]
    n_cols = n_heads * LANES
    heads = w_cols[..., :n_cols].reshape(lead + (n_heads, LANES))[..., perm]
    return jnp.concatenate([heads.reshape(lead + (n_cols,)), w_cols[..., n_cols:]], axis=-1)


def _axial_tables(seq):
    t = jnp.arange(seq)
    half = A_HEAD_DIM // 2
    cr, sr = _rope_cos_sin(t // GRID_W, half, A_ROPE_THETA)
    cc, sc = _rope_cos_sin(t % GRID_W, half, A_ROPE_THETA)
    return (jnp.concatenate([cr, cc, cr, cc], axis=-1),
            jnp.concatenate([-sr, -sc, sr, sc], axis=-1))


def _partial_tables(seq, rope_dim, theta):
    c, s = _rope_cos_sin(jnp.arange(seq), rope_dim, theta)
    fill = LANES // 2 - rope_dim // 2
    one, zero = jnp.ones((seq, fill), F32), jnp.zeros((seq, fill), F32)
    return (jnp.concatenate([c, one, c, one], axis=-1),
            jnp.concatenate([-s, zero, s, zero], axis=-1))


def _row(v):
    return v.reshape(1, -1).astype(F32)


def _prep_weights(w):
    out = {}
    out["w_mlp_in"] = w["w_mlp_in"].astype(BF16)
    out["w_mlp_out"] = w["w_mlp_out"].astype(BF16)
    out["w_ple"] = w["w_ple"].astype(BF16)
    out["w_ple_gate"] = w["w_ple_gate"].astype(BF16)
    out["a_w_qkv"] = _permute_heads(w["a_w_qkv"], A_HEADS + A_KV_HEADS, A_PERM).astype(BF16)
    out["a_w_o"] = w["a_w_o"].astype(BF16)
    out["a_head_g"] = jnp.concatenate(
        [jnp.tile(w["a_q_norm_g"][:, A_PERM] * (A_HEAD_DIM ** -0.5 * LOG2E), (1, A_HEADS)),
         jnp.tile(w["a_k_norm_g"][:, A_PERM], (1, A_KV_HEADS)),
         jnp.ones((w["a_q_norm_g"].shape[0], A_KV_HEADS * A_HEAD_DIM), F32)], axis=-1)

    n_b = w["b_w_dqkv"].shape[0]
    lat_w = B_Q_RANK + B_KV_RANK + B_ROPE_DIM
    n_lat_groups = B_LAT_PAD // LANES

    def permute_last_group(a, n_groups):
        lead = a.shape[:-1]
        grouped = a.reshape(lead + (n_groups, LANES))
        last = grouped[..., n_groups - 1:, :][..., B_PERM]
        return jnp.concatenate([grouped[..., :n_groups - 1, :], last], axis=-2).reshape(a.shape)

    dqkv = jnp.pad(w["b_w_dqkv"], ((0, 0), (0, 0), (0, B_LAT_PAD - lat_w)))
    out["b_w_dqkv"] = permute_last_group(dqkv, n_lat_groups).astype(BF16)
    uq = w["b_w_uq"].reshape(n_b, B_Q_RANK, B_HEADS, B_QK_DIM)
    uq = jnp.pad(uq, ((0, 0), (0, 0), (0, 0), (0, B_HEAD_PAD - B_QK_DIM)))
    uq = permute_last_group(uq, B_HEAD_PAD // LANES)
    out["b_w_uq"] = uq.reshape(n_b, B_Q_RANK, B_HEADS * B_HEAD_PAD).astype(BF16)
    ukv = w["b_w_ukv"].reshape(n_b, B_KV_RANK, B_HEADS, B_NOPE_DIM + B_V_DIM)
    out["b_w_uk"] = ukv[..., :B_NOPE_DIM].reshape(n_b, B_KV_RANK, B_HEADS * B_NOPE_DIM).astype(BF16)
    out["b_w_uv"] = ukv[..., B_NOPE_DIM:].reshape(n_b, B_KV_RANK, B_HEADS * B_V_DIM).astype(BF16)
    pad_g = ((0, 0), (0, B_HEAD_PAD - B_QK_DIM))
    out["b_q_g"] = permute_last_group(
        jnp.pad(w["b_q_norm_g"] * (B_QK_DIM ** -0.5 * LOG2E), pad_g), B_HEAD_PAD // LANES)
    out["b_k_g"] = permute_last_group(jnp.pad(w["b_k_norm_g"], pad_g), B_HEAD_PAD // LANES)
    out["b_w_o"] = w["b_w_o"].astype(BF16)

    n_ch = C_GROUPS * C_HEADS
    hw = C_HEADS * C_HEAD_DIM

    def group_major(a):
        lead = a.shape[:-1]
        a = a.reshape(lead + (3, C_GROUPS, hw))
        return jnp.swapaxes(a, -3, -2).reshape(lead + (3 * C_GROUPS * hw,))

    out["c_w_qkv"] = group_major(_permute_heads(w["c_w_qkv"], 2 * n_ch, C_PERM)).astype(BF16)
    out["c_w_o"] = w["c_w_o"].astype(BF16)
    out["c_head_g"] = group_major(jnp.concatenate(
        [jnp.tile(w["c_q_norm_g"][:, C_PERM] * (C_HEAD_DIM ** -0.5 * LOG2E), (1, n_ch)),
         jnp.tile(w["c_k_norm_g"][:, C_PERM], (1, n_ch)),
         jnp.ones((w["c_q_norm_g"].shape[0], n_ch * C_HEAD_DIM), F32)], axis=-1))
    return out


def _mixer_a(x, norm_g, w, pw, j, batch, seq):
    cos, sin = _axial_tables(seq)
    w_qkv = pw["a_w_qkv"][j]
    qk, vt = _qkv_proj(x, w_qkv, norm_g, _row(pw["a_head_g"][j]), cos, sin, seq=seq,
                       n_normed_cols=(A_HEADS + A_KV_HEADS) * A_HEAD_DIM, tm=512,
                       tn=w_qkv.shape[1], n_transposed_cols=A_KV_HEADS * A_HEAD_DIM)
    group = A_HEADS // A_KV_HEADS
    o = _dense_attention(qk, qk, vt, batch=batch, seq=seq, n_groups=A_KV_HEADS,
                         heads=group, kv_heads=1, dq=A_HEAD_DIM,
                         q_cb0=0, k_cb0=A_HEADS)
    return _out_proj_residual(o, pw["a_w_o"][j], x)


def _mixer_b(x, norm_g, w, pw, j, batch, seq):
    cos, sin = _partial_tables(seq, B_ROPE_DIM, B_ROPE_THETA)
    tm = 512
    n_pos_blocks = seq // tm
    lat = _fused_matmul(x, pw["b_w_dqkv"][j], tm=tm, tn=B_LAT_PAD, chunk=B_LAT_PAD // 3,
                        gain=norm_g, out_dtype=F32)

    def pos_spec():
        return pl.BlockSpec((tm, LANES), lambda i, jj: (i % n_pos_blocks, 0))

    head_g_spec = pl.BlockSpec((1, B_HEAD_PAD), lambda i, jj: (0, 0))
    q = _fused_matmul(lat, pw["b_w_uq"][j], tm=tm, tn=B_HEADS * B_HEAD_PAD, chunk=512,
                      gain=_row(w["b_cq_norm_g"][j]), x_kblock=0,
                      extras=[(_row(pw["b_q_g"][j]), head_g_spec), (cos, pos_spec()), (sin, pos_spec())],
                      epilogue=_mla_q_epilogue)
    ckv_g = _row(w["b_ckv_norm_g"][j])
    k_rope_block = (B_Q_RANK + B_KV_RANK) // LANES
    k = _fused_matmul(lat, pw["b_w_uk"][j], tm=tm, tn=B_HEADS * B_NOPE_DIM, chunk=512,
                      gain=ckv_g, x_kblock=1,
                      extras=[(lat, pl.BlockSpec((tm, LANES), lambda i, jj: (i, k_rope_block))),
                              (_row(pw["b_k_g"][j]), head_g_spec), (cos, pos_spec()), (sin, pos_spec())],
                      epilogue=_mla_k_epilogue, out_tile=B_HEADS * B_HEAD_PAD)
    vt = _fused_matmul(lat, pw["b_w_uv"][j], tm=tm, tn=B_HEADS * B_V_DIM, chunk=512,
                       gain=ckv_g, x_kblock=1, n_transposed_cols=B_HEADS * B_V_DIM)
    heads = 4
    o = _dense_attention(q, k, vt, batch=batch, seq=seq, n_groups=B_HEADS // heads,
                         heads=heads, kv_heads=heads, dq=B_HEAD_PAD,
                         q_cb0=0, k_cb0=0)
    return _out_proj_residual(o, pw["b_w_o"][j], x)


def _mixer_c(x, norm_g, w, pw, j, batch, seq):
    cos, sin = _partial_tables(seq, C_ROPE_DIM, C_ROPE_THETA)
    n_ch = C_GROUPS * C_HEADS
    by_group = _dilated_qkv_proj(x, pw["c_w_qkv"][j], norm_g, _row(pw["c_head_g"][j]), cos, sin,
                                 batch=batch, seq=seq)
    os, lses = [], []
    for xg, (_, dil) in zip(by_group, C_PAIRS):
        o, lse = _banded_attention(xg.reshape(batch * seq, xg.shape[-1]), seq_len=seq // dil)
        os.append(o)
        lses.append(lse)
    return _merge_out_proj_residual(os, lses, pw["c_w_o"][j], x, batch=batch, seq=seq)


def _layer_stack(x, p, w, pw):
    batch, seq, d = x.shape
    x = x.reshape(batch * seq, d)
    p = p.reshape(DEPTH, batch * seq, PLE_DIM)
    for i in range(DEPTH):
        kind, j = i % N_MIXERS, i // N_MIXERS
        mixer = (_mixer_a, _mixer_b, _mixer_c)[kind]
        x = mixer(x, _row(w["norm_mix_g"][i]), w, pw, j, batch, seq)
        x = _mlp_residual(x, _row(w["norm_mlp_g"][i]), pw["w_mlp_in"][i], pw["w_mlp_out"][i])
        x = _ple_residual(x, p[i], _row(w["ple_gate_norm_g"][i]), pw["w_ple_gate"][i],
                          pw["w_ple"][i], _row(w["ple_norm_g"][i]))
    return x.reshape(batch, seq, d)


def kernel(x_prompt, x_sample, p_prompt, p_sample, norm_mix_g, norm_mlp_g, w_mlp_in, w_mlp_out, w_ple, ple_norm_g, ple_gate_norm_g, w_ple_gate, a_w_qkv, a_q_norm_g, a_k_norm_g, a_w_o, b_w_dqkv, b_cq_norm_g, b_ckv_norm_g, b_w_uq, b_w_ukv, b_q_norm_g, b_k_norm_g, b_w_o, c_w_qkv, c_q_norm_g, c_k_norm_g, c_w_o):
    w = dict(norm_mix_g=norm_mix_g, norm_mlp_g=norm_mlp_g, w_mlp_in=w_mlp_in, w_mlp_out=w_mlp_out,
             w_ple=w_ple, ple_norm_g=ple_norm_g, ple_gate_norm_g=ple_gate_norm_g,
             w_ple_gate=w_ple_gate, a_w_qkv=a_w_qkv, a_q_norm_g=a_q_norm_g, a_k_norm_g=a_k_norm_g,
             a_w_o=a_w_o, b_w_dqkv=b_w_dqkv, b_cq_norm_g=b_cq_norm_g, b_ckv_norm_g=b_ckv_norm_g,
             b_w_uq=b_w_uq, b_w_ukv=b_w_ukv, b_q_norm_g=b_q_norm_g, b_k_norm_g=b_k_norm_g,
             b_w_o=b_w_o, c_w_qkv=c_w_qkv, c_q_norm_g=c_q_norm_g, c_k_norm_g=c_k_norm_g, c_w_o=c_w_o)
    pw = _prep_weights(w)
    return (_layer_stack(x_prompt, p_prompt, w, pw), _layer_stack(x_sample, p_sample, w, pw))
```

```python
import functools
import math

import numpy as np
import jax
import jax.numpy as jnp
from jax import lax
from jax.experimental import pallas as pl
from jax.experimental.pallas import tpu as pltpu

F32 = jnp.float32
BF16 = jnp.bfloat16

D_MODEL = 2048
DEPTH = 4
N_MIXERS = 3
PLE_DIM = 256
GRID_W = 64
EPS = 1e-6
D_FF = 4 * D_MODEL

A_HEADS = 16
A_KV_HEADS = 4
A_HEAD_DIM = 128
A_ROPE_THETA = 10000.0

B_HEADS = 16
B_Q_RANK = 512
B_KV_RANK = 512
B_NOPE_DIM = 128
B_ROPE_DIM = 64
B_V_DIM = 128
B_QK_DIM = B_NOPE_DIM + B_ROPE_DIM
B_ROPE_THETA = 10000.0
B_HEAD_PAD = 256
B_LAT_PAD = 1152

C_PAIRS = ((128, 1), (512, 4), (2048, 16))
C_GROUPS = 3
C_HEADS = 8
C_HEAD_DIM = 128
C_ROPE_DIM = 32
C_ROPE_THETA = 500000.0

LANES = 128
DEN_ROWS = 16
LOG2E = math.log2(math.e)
MASK_BIAS = -1e30
VMEM_LIMIT = 56 * 1024 * 1024


def _params(n_axes):
    return pltpu.CompilerParams(dimension_semantics=("arbitrary",) * n_axes,
                                vmem_limit_bytes=VMEM_LIMIT)


def _rms_to_bf16(x_ref, g_ref, xn_ref, rows=128):
    tm = x_ref.shape[0]

    def chunk(c, carry):
        r0 = pl.multiple_of(c * rows, rows)
        xf = x_ref[pl.ds(r0, rows), :]
        ms = jnp.mean(xf * xf, axis=-1, keepdims=True)
        xn_ref[pl.ds(r0, rows), :] = (xf * lax.rsqrt(ms + EPS) * g_ref[...]).astype(BF16)
        return carry

    lax.fori_loop(0, tm // rows, chunk, 0)


def _rope(y, cos, sin):
    return y * cos + pltpu.roll(y, LANES // 2, 1) * sin


def _fused_matmul(x, w, *, tm, tn, chunk, gain=None, x_kblock=0, extras=(), epilogue=None,
                  n_epilogue_cols=None, out_tile=None, out_dtype=BF16, n_transposed_cols=0):
    t_rows = x.shape[0]
    k_dim, n_dim = w.shape
    nj = n_dim // tn
    ow = out_tile or tn
    norm = gain is not None
    n_extra = len(extras)
    n_epi = 0 if epilogue is None else (n_dim if n_epilogue_cols is None else n_epilogue_cols)
    assert nj == 1 or n_epi % tn == 0
    assert n_epi == n_dim or ow == tn
    n_t = n_transposed_cols
    n_main = n_dim - n_t
    assert n_t == 0 or (nj == 1 and n_t % chunk == 0 and n_epi <= n_main and ow == tn)

    def body(*refs):
        x_ref, w_ref = refs[0], refs[1]
        pos = 2
        if norm:
            g_ref = refs[pos]
            pos += 1
        extra_refs = refs[pos:pos + n_extra]
        pos += n_extra
        o_ref = t_ref = None
        if n_main:
            o_ref = refs[pos]
            pos += 1
        if n_t:
            t_ref = refs[pos]
            pos += 1
        j = pl.program_id(1)
        if norm:
            lhs_ref = refs[pos]

            @pl.when(j == 0)
            def _():
                _rms_to_bf16(x_ref, g_ref, lhs_ref)
        else:
            lhs_ref = x_ref

        def run(with_epilogue):
            for c0 in range(0, tn, chunk):
                acc = jnp.dot(lhs_ref[...], w_ref[:, c0:c0 + chunk], preferred_element_type=F32)
                if c0 >= n_main:
                    t_ref[c0 - n_main:c0 - n_main + chunk, :] = acc.T.astype(BF16)
                elif with_epilogue(c0):
                    epilogue(acc, c0, extra_refs, o_ref)
                else:
                    o_ref[:, c0:c0 + chunk] = acc.astype(o_ref.dtype)

        if nj == 1:
            run(lambda c0: c0 < n_epi)
        elif n_epi in (0, n_dim):
            run(lambda c0: n_epi > 0)
        else:
            @pl.when(j < n_epi // tn)
            def _():
                run(lambda c0: True)

            @pl.when(j >= n_epi // tn)
            def _():
                run(lambda c0: False)

    in_specs = [pl.BlockSpec((tm, k_dim), lambda i, j: (i, x_kblock)),
                pl.BlockSpec((k_dim, tn), lambda i, j: (0, j))]
    args = [x, w]
    if norm:
        in_specs.append(pl.BlockSpec((1, k_dim), lambda i, j: (0, 0)))
        args.append(gain)
    for arr, spec in extras:
        in_specs.append(spec)
        args.append(arr)
    scratch = [pltpu.VMEM((tm, k_dim), BF16)] if norm else []
    out_specs, out_shape = [], []
    if n_main:
        main_w = ow if n_t == 0 else n_main
        out_specs.append(pl.BlockSpec((tm, main_w), lambda i, j: (i, j)))
        out_shape.append(jax.ShapeDtypeStruct((t_rows, nj * main_w), out_dtype))
    if n_t:
        out_specs.append(pl.BlockSpec((n_t, tm), lambda i, j: (0, i)))
        out_shape.append(jax.ShapeDtypeStruct((n_t, t_rows), BF16))
    if len(out_specs) == 1:
        out_specs, out_shape = out_specs[0], out_shape[0]
    return pl.pallas_call(
        body,
        grid=(t_rows // tm, nj),
        in_specs=in_specs,
        out_specs=out_specs,
        out_shape=out_shape,
        scratch_shapes=scratch,
        compiler_params=_params(2),
    )(*args)


def _residual_epilogue(acc, c0, extra_refs, o_ref):
    (res_ref,) = extra_refs
    cols = slice(c0, c0 + acc.shape[1])
    o_ref[:, cols] = res_ref[:, cols] + acc


def _out_proj_residual(o, w_o, x, *, tm=512):
    n_dim = w_o.shape[1]
    extras = [(x, pl.BlockSpec((tm, n_dim), lambda i, j: (i, 0)))]
    return _fused_matmul(o, w_o, tm=tm, tn=n_dim, chunk=512, extras=extras,
                         epilogue=_residual_epilogue, out_dtype=F32)


def _lane_sum(v):
    hi = v.astype(BF16)
    lo = (v - hi.astype(F32)).astype(BF16)
    ones = jnp.ones((2 * LANES, LANES), BF16)
    return jnp.dot(jnp.concatenate([hi, lo], axis=1), ones, preferred_element_type=F32)


def _head_norm_rope(c, gain, cos, sin):
    ms = _lane_sum(c * c) * (1.0 / LANES)
    return _rope(c * lax.rsqrt(ms + EPS) * gain, cos, sin)


def _head_norm_rope_epilogue(acc, c0, extra_refs, o_ref):
    g_ref, cos_ref, sin_ref = extra_refs
    for h in range(acc.shape[1] // LANES):
        cols = slice(c0 + h * LANES, c0 + (h + 1) * LANES)
        y = _head_norm_rope(acc[:, h * LANES:(h + 1) * LANES], g_ref[:, cols],
                            cos_ref[...], sin_ref[...])
        o_ref[:, cols] = y.astype(o_ref.dtype)


def _qkv_proj(x, w, norm_g, head_g, cos, sin, *, seq, n_normed_cols, tm, tn, n_transposed_cols=0):
    n_pos_blocks = seq // tm
    extras = [
        (head_g, pl.BlockSpec((1, tn), lambda i, j: (0, j))),
        (cos, pl.BlockSpec((tm, LANES), lambda i, j: (i % n_pos_blocks, 0))),
        (sin, pl.BlockSpec((tm, LANES), lambda i, j: (i % n_pos_blocks, 0))),
    ]
    return _fused_matmul(x, w, tm=tm, tn=tn, chunk=512, gain=norm_g, extras=extras,
                         epilogue=_head_norm_rope_epilogue, n_epilogue_cols=n_normed_cols,
                         n_transposed_cols=n_transposed_cols)


def _mla_q_epilogue(acc, c0, extra_refs, o_ref):
    g_ref, cos_ref, sin_ref = extra_refs
    for h in range(acc.shape[1] // B_HEAD_PAD):
        c_lo = acc[:, h * B_HEAD_PAD:h * B_HEAD_PAD + LANES]
        c_hi = acc[:, h * B_HEAD_PAD + LANES:(h + 1) * B_HEAD_PAD]
        ss = _lane_sum(c_lo * c_lo + c_hi * c_hi) * (1.0 / B_QK_DIM)
        r = lax.rsqrt(ss + EPS)
        base = c0 + h * B_HEAD_PAD
        o_ref[:, base:base + LANES] = (c_lo * r * g_ref[:, :LANES]).astype(o_ref.dtype)
        y_hi = c_hi * r * g_ref[:, LANES:]
        o_ref[:, base + LANES:base + B_HEAD_PAD] = _rope(
            y_hi, cos_ref[...], sin_ref[...]).astype(o_ref.dtype)


def _mla_k_epilogue(acc, c0, extra_refs, o_ref):
    kr_ref, g_ref, cos_ref, sin_ref = extra_refs
    kr = kr_ref[...]
    kr_sq = kr * kr
    for h in range(acc.shape[1] // LANES):
        c = acc[:, h * LANES:(h + 1) * LANES]
        ss = _lane_sum(c * c + kr_sq) * (1.0 / B_QK_DIM)
        r = lax.rsqrt(ss + EPS)
        base = (c0 // LANES + h) * B_HEAD_PAD
        o_ref[:, base:base + LANES] = (c * r * g_ref[:, :LANES]).astype(o_ref.dtype)
        y_hi = kr * r * g_ref[:, LANES:]
        o_ref[:, base + LANES:base + B_HEAD_PAD] = _rope(
            y_hi, cos_ref[...], sin_ref[...]).astype(o_ref.dtype)


def _dense_attention(q_arr, k_arr, vt_arr, *, batch, seq, n_groups, heads, kv_heads, dq,
                     q_cb0, k_cb0, tq=1024, tk=1024):
    nq, nk = seq // tq, seq // tk
    dv = LANES

    def body(q_ref, k_ref, vt_ref, o_ref, qt_ref, m_ref, acc_ref, s_ref, mc_ref):
        ki = pl.program_id(3)

        def score_phase(slot):
            for g in range(heads):
                kv = g * kv_heads // heads
                s = jnp.dot(k_ref[:, kv * dq:(kv + 1) * dq], qt_ref[g],
                            preferred_element_type=F32)
                s_ref[slot, g] = s
                mc_ref[slot, g] = jnp.max(s, axis=0, keepdims=True)

        def value_phase(slot):
            ones_rows = jnp.ones((DEN_ROWS, tk), BF16)
            for g in range(heads):
                kv = g * kv_heads // heads
                m_prev = m_ref[g]
                m_new = jnp.maximum(m_prev, mc_ref[slot, g])
                alpha = jnp.exp2(m_prev - m_new)
                p = jnp.exp2((s_ref[slot, g] - m_new).astype(BF16))
                vt_ext = jnp.concatenate([vt_ref[kv * dv:(kv + 1) * dv, :], ones_rows], axis=0)
                pv = jnp.dot(vt_ext, p, preferred_element_type=F32)
                acc_ref[g] = acc_ref[g] * alpha + pv
                m_ref[g] = m_new

        @pl.when(ki == 0)
        def _():
            for g in range(heads):
                qg = q_ref[:, g * dq:(g + 1) * dq].astype(F32)
                qt_ref[g] = qg.T.astype(BF16)
            m_ref[...] = jnp.full(m_ref.shape, -jnp.inf, F32)
            acc_ref[...] = jnp.zeros(acc_ref.shape, F32)
            score_phase(0)

        for parity in (0, 1):
            @pl.when((ki > 0) & (ki < nk) & (ki % 2 == parity))
            def _():
                score_phase(parity)
                value_phase(1 - parity)

        @pl.when(ki == nk)
        def _():
            value_phase((nk - 1) % 2)
            for g in range(heads):
                o = acc_ref[g, :dv, :] / acc_ref[g, dv:dv + 1, :]
                o_ref[:, g * dv:(g + 1) * dv] = o.T.astype(o_ref.dtype)

    t_rows = batch * seq
    return pl.pallas_call(
        body,
        grid=(batch, n_groups, nq, nk + 1),
        in_specs=[
            pl.BlockSpec((tq, heads * dq), lambda b, g, qi, ki: (b * nq + qi, q_cb0 + g)),
            pl.BlockSpec((tk, kv_heads * dq),
                         lambda b, g, qi, ki: (b * nk + jnp.minimum(ki, nk - 1), k_cb0 + g)),
            pl.BlockSpec((kv_heads * dv, tk),
                         lambda b, g, qi, ki: (g, b * nk + jnp.maximum(ki - 1, 0))),
        ],
        out_specs=pl.BlockSpec((tq, heads * dv), lambda b, g, qi, ki: (b * nq + qi, g)),
        out_shape=jax.ShapeDtypeStruct((t_rows, n_groups * heads * dv), BF16),
        scratch_shapes=[
            pltpu.VMEM((heads, dq, tq), BF16),
            pltpu.VMEM((heads, 1, tq), F32),
            pltpu.VMEM((heads, dv + DEN_ROWS, tq), F32),
            pltpu.VMEM((2, heads, tk, tq), F32),
            pltpu.VMEM((2, heads, 1, tq), F32),
        ],
        compiler_params=_params(4),
    )(q_arr, k_arr, vt_arr)


C_TL = 256
C_REACH = 64
assert all(window // (2 * dil) == C_REACH for window, dil in C_PAIRS)
assert tuple(dil for _, dil in C_PAIRS) == (1, 4, 16)


def _band_bias(tl):
    i = np.arange(tl)[None, :]
    jh = np.arange(C_REACH)[:, None]
    jc = np.arange(tl)[:, None]

    def as_bias(ok):
        return jnp.asarray(np.where(ok, 0.0, MASK_BIAS), F32)

    return as_bias(jh >= i), as_bias(np.abs(jc - i) <= C_REACH), as_bias(i >= tl - C_REACH + jh)


def _dilated_qkv_proj(x, w, norm_g, head_g, cos, sin, *, batch, seq, tm=1024, chunk=512):
    t_rows, k_dim = x.shape
    hw = C_HEADS * C_HEAD_DIM
    n_steps = 3 * C_GROUPS
    blocks_per_seq = seq // tm
    dils = [dil for _, dil in C_PAIRS]
    assert all(tm % dil == 0 and (tm // dil) % 16 == 0 for dil in dils)

    def body(x_ref, w_ref, g_ref, hg_ref, cos_ref, sin_ref, o0_ref, o1_ref, o2_ref, xn_ref, res_ref):
        j = pl.program_id(1)

        @pl.when(j == 0)
        def _():
            _rms_to_bf16(x_ref, g_ref, xn_ref)

        def step(o_ref, dil, with_epilogue):
            for c0 in range(0, hw, chunk):
                acc = jnp.dot(xn_ref[...], w_ref[:, c0:c0 + chunk], preferred_element_type=F32)
                for h in range(chunk // LANES):
                    c = acc[:, h * LANES:(h + 1) * LANES]
                    head = c0 // LANES + h
                    cols = slice(head * LANES, (head + 1) * LANES)
                    if with_epilogue:
                        c = _head_norm_rope(c, hg_ref[:, cols], cos_ref[...], sin_ref[...])
                    if dil == 1:
                        o_ref[0, :, cols] = c.astype(BF16)
                        continue
                    res_ref[head] = c
                    for r in range(dil):
                        o_ref[r, :, cols] = res_ref[
                            head, pl.ds(r, tm // dil, stride=dil), :].astype(BF16)

        for g, (o_ref, dil) in enumerate(zip((o0_ref, o1_ref, o2_ref), dils)):
            @pl.when((j // 3 == g) & (j % 3 < 2))
            def _():
                step(o_ref, dil, True)

            @pl.when((j // 3 == g) & (j % 3 == 2))
            def _():
                step(o_ref, dil, False)

    def out_spec(g, dil):
        return pl.BlockSpec(
            (dil, tm // dil, hw),
            lambda i, j: (i // blocks_per_seq, i % blocks_per_seq, jnp.clip(j - 3 * g, 0, 2)))

    return pl.pallas_call(
        body,
        grid=(t_rows // tm, n_steps),
        in_specs=[
            pl.BlockSpec((tm, k_dim), lambda i, j: (i, 0)),
            pl.BlockSpec((k_dim, hw), lambda i, j: (0, j)),
            pl.BlockSpec((1, k_dim), lambda i, j: (0, 0)),
            pl.BlockSpec((1, hw), lambda i, j: (0, j)),
            pl.BlockSpec((tm, LANES), lambda i, j: (i % blocks_per_seq, 0)),
            pl.BlockSpec((tm, LANES), lambda i, j: (i % blocks_per_seq, 0)),
        ],
        out_specs=[out_spec(g, dil) for g, dil in enumerate(dils)],
        out_shape=[jax.ShapeDtypeStruct((batch * dil, seq // dil, 3 * hw), BF16) for dil in dils],
        scratch_shapes=[pltpu.VMEM((tm, k_dim), BF16), pltpu.VMEM((C_HEADS, tm, LANES), F32)],
        compiler_params=_params(2),
    )(x, w, norm_g, head_g, cos, sin)


def _banded_attention(x, *, seq_len, tl=C_TL):
    hw = C_HEADS * C_HEAD_DIM
    dh = C_HEAD_DIM
    n_rows = x.shape[0]
    halo_per_block = tl // C_REACH
    n_halo_blocks = n_rows // C_REACH
    nb_seq = seq_len // tl
    assert seq_len % tl == 0
    bias_prev, bias_cur, bias_next = _band_bias(tl)
    nt_dims = (((1,), (1,)), ((), ()))
    tn_dims = (((0,), (0,)), ((), ()))

    def body(q_ref, kp_ref, kc_ref, kn_ref, vp_ref, vc_ref, vn_ref, bp_ref, bc_ref, bn_ref,
             o_ref, lse_ref):
        pos = pl.program_id(0) % nb_seq
        b_prev = bp_ref[...] + jnp.where(pos == 0, MASK_BIAS, 0.0)
        b_next = bn_ref[...] + jnp.where(pos == nb_seq - 1, MASK_BIAS, 0.0)
        b_cur = bc_ref[...]
        lse_rows = []
        for h in range(C_HEADS):
            sl = slice(h * dh, (h + 1) * dh)
            q = q_ref[:, sl]
            s_p = lax.dot_general(kp_ref[:, sl], q, nt_dims, preferred_element_type=F32) + b_prev
            s_c = lax.dot_general(kc_ref[:, sl], q, nt_dims, preferred_element_type=F32) + b_cur
            s_n = lax.dot_general(kn_ref[:, sl], q, nt_dims, preferred_element_type=F32) + b_next
            m = jnp.maximum(jnp.max(s_c, axis=0, keepdims=True),
                            jnp.maximum(jnp.max(s_p, axis=0, keepdims=True),
                                        jnp.max(s_n, axis=0, keepdims=True)))
            p_p = jnp.exp2(s_p - m)
            p_c = jnp.exp2(s_c - m)
            p_n = jnp.exp2(s_n - m)
            l = (jnp.sum(p_c, axis=0, keepdims=True) + jnp.sum(p_p, axis=0, keepdims=True)
                 + jnp.sum(p_n, axis=0, keepdims=True))
            ot = (lax.dot_general(vc_ref[:, sl], p_c.astype(BF16), tn_dims, preferred_element_type=F32)
                  + lax.dot_general(vp_ref[:, sl], p_p.astype(BF16), tn_dims, preferred_element_type=F32)
                  + lax.dot_general(vn_ref[:, sl], p_n.astype(BF16), tn_dims, preferred_element_type=F32))
            o_ref[:, sl] = (ot / l).T
            lse_rows.append((m + jnp.log2(l)) * (1.0 / LOG2E))
        lse_t = jnp.concatenate(lse_rows + [jnp.zeros((LANES - C_HEADS, tl), F32)], axis=0)
        lse_ref[...] = lse_t.T

    def prev_halo(blk):
        return jnp.maximum(blk * halo_per_block - 1, 0)

    def next_halo(blk):
        return jnp.minimum((blk + 1) * halo_per_block, n_halo_blocks - 1)

    const = lambda blk: (0, 0)
    return pl.pallas_call(
        body,
        grid=(n_rows // tl,),
        in_specs=[
            pl.BlockSpec((tl, hw), lambda blk: (blk, 0)),
            pl.BlockSpec((C_REACH, hw), lambda blk: (prev_halo(blk), 1)),
            pl.BlockSpec((tl, hw), lambda blk: (blk, 1)),
            pl.BlockSpec((C_REACH, hw), lambda blk: (next_halo(blk), 1)),
            pl.BlockSpec((C_REACH, hw), lambda blk: (prev_halo(blk), 2)),
            pl.BlockSpec((tl, hw), lambda blk: (blk, 2)),
            pl.BlockSpec((C_REACH, hw), lambda blk: (next_halo(blk), 2)),
            pl.BlockSpec((C_REACH, tl), const),
            pl.BlockSpec((tl, tl), const),
            pl.BlockSpec((C_REACH, tl), const),
        ],
        out_specs=[pl.BlockSpec((tl, hw), lambda blk: (blk, 0)),
                   pl.BlockSpec((tl, LANES), lambda blk: (blk, 0))],
        out_shape=[jax.ShapeDtypeStruct((n_rows, hw), F32),
                   jax.ShapeDtypeStruct((n_rows, LANES), F32)],
        compiler_params=_params(1),
    )(x, x, x, x, x, x, x, bias_prev, bias_cur, bias_next)


def _merge_out_proj_residual(os, lses, w_o, x, *, batch, seq, tm=512, tn=512):
    t_rows = x.shape[0]
    hw, n_dim = w_o.shape
    dh = C_HEAD_DIM
    dils = [dil for _, dil in C_PAIRS]
    blocks_per_seq = seq // tm
    assert all(tm % dil == 0 and (tm // dil) % 8 == 0 for dil in dils)

    def body(o0_ref, o1_ref, o2_ref, l0_ref, l1_ref, l2_ref, w_ref, res_ref, out_ref,
             ot_ref, lt_ref, mg_ref):
        @pl.when(pl.program_id(1) == 0)
        def _():
            for g, (o_ref, l_ref, dil) in enumerate(zip((o0_ref, o1_ref, o2_ref),
                                                        (l0_ref, l1_ref, l2_ref), dils)):
                for r in range(dil):
                    rows = pl.ds(r, tm // dil, stride=dil)
                    for h in range(C_HEADS):
                        ot_ref[g, h, rows, :] = o_ref[r, :, h * dh:(h + 1) * dh]
                    lt_ref[g, rows, :] = l_ref[r]
            lse = [lt_ref[g] for g in range(C_GROUPS)]
            mx = jnp.maximum(lse[0], jnp.maximum(lse[1], lse[2]))
            e = [jnp.exp(v - mx) for v in lse]
            den = e[0] + e[1] + e[2]
            wts = [v / den for v in e]
            for h in range(C_HEADS):
                sl = slice(h * dh, (h + 1) * dh)
                mg = (wts[0][:, h:h + 1] * ot_ref[0, h] + wts[1][:, h:h + 1] * ot_ref[1, h]
                      + wts[2][:, h:h + 1] * ot_ref[2, h])
                mg_ref[:, sl] = mg.astype(BF16)

        out_ref[...] = res_ref[...] + jnp.dot(mg_ref[...], w_ref[...], preferred_element_type=F32)

    def group_spec(dil, width):
        return pl.BlockSpec((dil, tm // dil, width),
                            lambda i, j: (i // blocks_per_seq, i % blocks_per_seq, 0))

    args = ([o.reshape(batch * dil, seq // dil, hw) for o, dil in zip(os, dils)]
            + [l.reshape(batch * dil, seq // dil, LANES) for l, dil in zip(lses, dils)])
    return pl.pallas_call(
        body,
        grid=(t_rows // tm, n_dim // tn),
        in_specs=([group_spec(dil, hw) for dil in dils] + [group_spec(dil, LANES) for dil in dils]
                  + [pl.BlockSpec((hw, tn), lambda i, j: (0, j)),
                     pl.BlockSpec((tm, tn), lambda i, j: (i, j))]),
        out_specs=pl.BlockSpec((tm, tn), lambda i, j: (i, j)),
        out_shape=jax.ShapeDtypeStruct((t_rows, n_dim), F32),
        scratch_shapes=[pltpu.VMEM((C_GROUPS, C_HEADS, tm, dh), F32),
                        pltpu.VMEM((C_GROUPS, tm, LANES), F32),
                        pltpu.VMEM((tm, hw), BF16)],
        compiler_params=_params(2),
    )(*args, w_o, x)


def _mlp_residual(x, gain, w_in, w_out, *, tm=1024, tf=512):
    t_rows, d = x.shape
    nf = w_in.shape[1] // tf

    def body(x_ref, g_ref, wi_ref, wo_ref, o_ref, xn_ref):
        @pl.when(pl.program_id(1) == 0)
        def _():
            _rms_to_bf16(x_ref, g_ref, xn_ref)
            o_ref[...] = x_ref[...]

        h = jnp.dot(xn_ref[...], wi_ref[...], preferred_element_type=F32)
        h = jnp.square(jnp.maximum(h, 0.0)).astype(BF16)
        o_ref[...] += jnp.dot(h, wo_ref[...], preferred_element_type=F32)

    return pl.pallas_call(
        body,
        grid=(t_rows // tm, nf),
        in_specs=[
            pl.BlockSpec((tm, d), lambda i, f: (i, 0)),
            pl.BlockSpec((1, d), lambda i, f: (0, 0)),
            pl.BlockSpec((d, tf), lambda i, f: (0, f)),
            pl.BlockSpec((tf, d), lambda i, f: (f, 0)),
        ],
        out_specs=pl.BlockSpec((tm, d), lambda i, f: (i, 0)),
        out_shape=jax.ShapeDtypeStruct((t_rows, d), F32),
        scratch_shapes=[pltpu.VMEM((tm, d), BF16)],
        compiler_params=_params(2),
    )(x, gain, w_in, w_out)


def _ple_residual(x, p, gate_g, w_gate, w_ple, ple_g, *, tm=512):
    t_rows, d = x.shape
    pd = p.shape[1]

    def body(x_ref, p_ref, gg_ref, wg_ref, wp_ref, pg_ref, o_ref, xn_ref):
        _rms_to_bf16(x_ref, gg_ref, xn_ref)
        e = jnp.dot(p_ref[...].astype(BF16), wp_ref[...], preferred_element_type=F32)
        ms = jnp.mean(e * e, axis=-1, keepdims=True)
        e = e * lax.rsqrt(ms + EPS) * pg_ref[...]
        z = jnp.dot(xn_ref[...], wg_ref[...], preferred_element_type=F32)
        o_ref[...] = x_ref[...] + jax.nn.sigmoid(z) * e

    const = lambda i: (0, 0)
    return pl.pallas_call(
        body,
        grid=(t_rows // tm,),
        in_specs=[
            pl.BlockSpec((tm, d), lambda i: (i, 0)),
            pl.BlockSpec((tm, pd), lambda i: (i, 0)),
            pl.BlockSpec((1, d), const),
            pl.BlockSpec((d, d), const, pipeline_mode=pl.Buffered(1)),
            pl.BlockSpec((pd, d), const, pipeline_mode=pl.Buffered(1)),
            pl.BlockSpec((1, d), const),
        ],
        out_specs=pl.BlockSpec((tm, d), lambda i: (i, 0)),
        out_shape=jax.ShapeDtypeStruct((t_rows, d), F32),
        scratch_shapes=[pltpu.VMEM((tm, d), BF16)],
        compiler_params=_params(1),
    )(x, p, gate_g, w_gate, w_ple, ple_g)


def _rope_cos_sin(pos, dim, theta):
    inv_freq = theta ** (-jnp.arange(0, dim, 2, dtype=F32) / dim)
    ang = pos.astype(F32)[:, None] * inv_freq[None, :]
    return jnp.cos(ang), jnp.sin(ang)


def _pair_perm(first, second, rest):
    fill = LANES // 2 - len(first)
    perm = list(first) + list(rest[:fill]) + list(second) + list(rest[fill:])
    assert sorted(perm) == list(range(LANES))
    return np.asarray(perm)


A_PERM = _pair_perm(list(range(0, 32)) + list(range(64, 96)),
                    list(range(32, 64)) + list(range(96, 128)), [])
B_PERM = _pair_perm(range(0, 32), range(32, 64), list(range(64, 128)))
C_PERM = _pair_perm(range(0, 16), range(16, 32), list(range(32, 128)))


def _permute_heads(w_cols, n_heads, perm):
    lead = w_cols.shape[:-1]
    n_cols = n_heads * LANES
    heads = w_cols[..., :n_cols].reshape(lead + (n_heads, LANES))[..., perm]
    return jnp.concatenate([heads.reshape(lead + (n_cols,)), w_cols[..., n_cols:]], axis=-1)


def _axial_tables(seq):
    t = jnp.arange(seq)
    half = A_HEAD_DIM // 2
    cr, sr = _rope_cos_sin(t // GRID_W, half, A_ROPE_THETA)
    cc, sc = _rope_cos_sin(t % GRID_W, half, A_ROPE_THETA)
    return (jnp.concatenate([cr, cc, cr, cc], axis=-1),
            jnp.concatenate([-sr, -sc, sr, sc], axis=-1))


def _partial_tables(seq, rope_dim, theta):
    c, s = _rope_cos_sin(jnp.arange(seq), rope_dim, theta)
    fill = LANES // 2 - rope_dim // 2
    one, zero = jnp.ones((seq, fill), F32), jnp.zeros((seq, fill), F32)
    return (jnp.concatenate([c, one, c, one], axis=-1),
            jnp.concatenate([-s, zero, s, zero], axis=-1))


def _row(v):
    return v.reshape(1, -1).astype(F32)


def _prep_weights(w):
    out = {}
    out["w_mlp_in"] = w["w_mlp_in"].astype(BF16)
    out["w_mlp_out"] = w["w_mlp_out"].astype(BF16)
    out["w_ple"] = w["w_ple"].astype(BF16)
    out["w_ple_gate"] = w["w_ple_gate"].astype(BF16)
    out["a_w_qkv"] = _permute_heads(w["a_w_qkv"], A_HEADS + A_KV_HEADS, A_PERM).astype(BF16)
    out["a_w_o"] = w["a_w_o"].astype(BF16)
    out["a_head_g"] = jnp.concatenate(
        [jnp.tile(w["a_q_norm_g"][:, A_PERM] * (A_HEAD_DIM ** -0.5 * LOG2E), (1, A_HEADS)),
         jnp.tile(w["a_k_norm_g"][:, A_PERM], (1, A_KV_HEADS)),
         jnp.ones((w["a_q_norm_g"].shape[0], A_KV_HEADS * A_HEAD_DIM), F32)], axis=-1)

    n_b = w["b_w_dqkv"].shape[0]
    lat_w = B_Q_RANK + B_KV_RANK + B_ROPE_DIM
    n_lat_groups = B_LAT_PAD // LANES

    def permute_last_group(a, n_groups):
        lead = a.shape[:-1]
        grouped = a.reshape(lead + (n_groups, LANES))
        last = grouped[..., n_groups - 1:, :][..., B_PERM]
        return jnp.concatenate([grouped[..., :n_groups - 1, :], last], axis=-2).reshape(a.shape)

    dqkv = jnp.pad(w["b_w_dqkv"], ((0, 0), (0, 0), (0, B_LAT_PAD - lat_w)))
    out["b_w_dqkv"] = permute_last_group(dqkv, n_lat_groups).astype(BF16)
    uq = w["b_w_uq"].reshape(n_b, B_Q_RANK, B_HEADS, B_QK_DIM)
    uq = jnp.pad(uq, ((0, 0), (0, 0), (0, 0), (0, B_HEAD_PAD - B_QK_DIM)))
    uq = permute_last_group(uq, B_HEAD_PAD // LANES)
    out["b_w_uq"] = uq.reshape(n_b, B_Q_RANK, B_HEADS * B_HEAD_PAD).astype(BF16)
    ukv = w["b_w_ukv"].reshape(n_b, B_KV_RANK, B_HEADS, B_NOPE_DIM + B_V_DIM)
    out["b_w_uk"] = ukv[..., :B_NOPE_DIM].reshape(n_b, B_KV_RANK, B_HEADS * B_NOPE_DIM).astype(BF16)
    out["b_w_uv"] = ukv[..., B_NOPE_DIM:].reshape(n_b, B_KV_RANK, B_HEADS * B_V_DIM).astype(BF16)
    pad_g = ((0, 0), (0, B_HEAD_PAD - B_QK_DIM))
    out["b_q_g"] = permute_last_group(
        jnp.pad(w["b_q_norm_g"] * (B_QK_DIM ** -0.5 * LOG2E), pad_g), B_HEAD_PAD // LANES)
    out["b_k_g"] = permute_last_group(jnp.pad(w["b_k_norm_g"], pad_g), B_HEAD_PAD // LANES)
    out["b_w_o"] = w["b_w_o"].astype(BF16)

    n_ch = C_GROUPS * C_HEADS
    hw = C_HEADS * C_HEAD_DIM

    def group_major(a):
        lead = a.shape[:-1]
        a = a.reshape(lead + (3, C_GROUPS, hw))
        return jnp.swapaxes(a, -3, -2).reshape(lead + (3 * C_GROUPS * hw,))

    out["c_w_qkv"] = group_major(_permute_heads(w["c_w_qkv"], 2 * n_ch, C_PERM)).astype(BF16)
    out["c_w_o"] = w["c_w_o"].astype(BF16)
    out["c_head_g"] = group_major(jnp.concatenate(
        [jnp.tile(w["c_q_norm_g"][:, C_PERM] * (C_HEAD_DIM ** -0.5 * LOG2E), (1, n_ch)),
         jnp.tile(w["c_k_norm_g"][:, C_PERM], (1, n_ch)),
         jnp.ones((w["c_q_norm_g"].shape[0], n_ch * C_HEAD_DIM), F32)], axis=-1))
    return out


def _mixer_a(x, norm_g, w, pw, j, batch, seq):
    cos, sin = _axial_tables(seq)
    w_qkv = pw["a_w_qkv"][j]
    qk, vt = _qkv_proj(x, w_qkv, norm_g, _row(pw["a_head_g"][j]), cos, sin, seq=seq,
                       n_normed_cols=(A_HEADS + A_KV_HEADS) * A_HEAD_DIM, tm=512,
                       tn=w_qkv.shape[1], n_transposed_cols=A_KV_HEADS * A_HEAD_DIM)
    group = A_HEADS // A_KV_HEADS
    o = _dense_attention(qk, qk, vt, batch=batch, seq=seq, n_groups=A_KV_HEADS,
                         heads=group, kv_heads=1, dq=A_HEAD_DIM,
                         q_cb0=0, k_cb0=A_HEADS)
    return _out_proj_residual(o, pw["a_w_o"][j], x)


def _mixer_b(x, norm_g, w, pw, j, batch, seq):
    cos, sin = _partial_tables(seq, B_ROPE_DIM, B_ROPE_THETA)
    tm = 512
    n_pos_blocks = seq // tm
    lat = _fused_matmul(x, pw["b_w_dqkv"][j], tm=tm, tn=B_LAT_PAD, chunk=B_LAT_PAD // 3,
                        gain=norm_g, out_dtype=F32)

    def pos_spec():
        return pl.BlockSpec((tm, LANES), lambda i, jj: (i % n_pos_blocks, 0))

    head_g_spec = pl.BlockSpec((1, B_HEAD_PAD), lambda i, jj: (0, 0))
    q = _fused_matmul(lat, pw["b_w_uq"][j], tm=tm, tn=B_HEADS * B_HEAD_PAD, chunk=512,
                      gain=_row(w["b_cq_norm_g"][j]), x_kblock=0,
                      extras=[(_row(pw["b_q_g"][j]), head_g_spec), (cos, pos_spec()), (sin, pos_spec())],
                      epilogue=_mla_q_epilogue)
    ckv_g = _row(w["b_ckv_norm_g"][j])
    k_rope_block = (B_Q_RANK + B_KV_RANK) // LANES
    k = _fused_matmul(lat, pw["b_w_uk"][j], tm=tm, tn=B_HEADS * B_NOPE_DIM, chunk=512,
                      gain=ckv_g, x_kblock=1,
                      extras=[(lat, pl.BlockSpec((tm, LANES), lambda i, jj: (i, k_rope_block))),
                              (_row(pw["b_k_g"][j]), head_g_spec), (cos, pos_spec()), (sin, pos_spec())],
                      epilogue=_mla_k_epilogue, out_tile=B_HEADS * B_HEAD_PAD)
    vt = _fused_matmul(lat, pw["b_w_uv"][j], tm=tm, tn=B_HEADS * B_V_DIM, chunk=512,
                       gain=ckv_g, x_kblock=1, n_transposed_cols=B_HEADS * B_V_DIM)
    heads = 4
    o = _dense_attention(q, k, vt, batch=batch, seq=seq, n_groups=B_HEADS // heads,
                         heads=heads, kv_heads=heads, dq=B_HEAD_PAD,
                         q_cb0=0, k_cb0=0)
    return _out_proj_residual(o, pw["b_w_o"][j], x)


def _mixer_c(x, norm_g, w, pw, j, batch, seq):
    cos, sin = _partial_tables(seq, C_ROPE_DIM, C_ROPE_THETA)
    n_ch = C_GROUPS * C_HEADS
    by_group = _dilated_qkv_proj(x, pw["c_w_qkv"][j], norm_g, _row(pw["c_head_g"][j]), cos, sin,
                                 batch=batch, seq=seq)
    os, lses = [], []
    for xg, (_, dil) in zip(by_group, C_PAIRS):
        o, lse = _banded_attention(xg.reshape(batch * seq, xg.shape[-1]), seq_len=seq // dil)
        os.append(o)
        lses.append(lse)
    return _merge_out_proj_residual(os, lses, pw["c_w_o"][j], x, batch=batch, seq=seq)


def _layer_stack(x, p, w, pw):
    batch, seq, d = x.shape
    x = x.reshape(batch * seq, d)
    p = p.reshape(DEPTH, batch * seq, PLE_DIM)
    for i in range(DEPTH):
        kind, j = i % N_MIXERS, i // N_MIXERS
        mixer = (_mixer_a, _mixer_b, _mixer_c)[kind]
        x = mixer(x, _row(w["norm_mix_g"][i]), w, pw, j, batch, seq)
        x = _mlp_residual(x, _row(w["norm_mlp_g"][i]), pw["w_mlp_in"][i], pw["w_mlp_out"][i])
        x = _ple_residual(x, p[i], _row(w["ple_gate_norm_g"][i]), pw["w_ple_gate"][i],
                          pw["w_ple"][i], _row(w["ple_norm_g"][i]))
    return x.reshape(batch, seq, d)


def kernel(x_prompt, x_sample, p_prompt, p_sample, norm_mix_g, norm_mlp_g, w_mlp_in, w_mlp_out, w_ple, ple_norm_g, ple_gate_norm_g, w_ple_gate, a_w_qkv, a_q_norm_g, a_k_norm_g, a_w_o, b_w_dqkv, b_cq_norm_g, b_ckv_norm_g, b_w_uq, b_w_ukv, b_q_norm_g, b_k_norm_g, b_w_o, c_w_qkv, c_q_norm_g, c_k_norm_g, c_w_o):
    w = dict(norm_mix_g=norm_mix_g, norm_mlp_g=norm_mlp_g, w_mlp_in=w_mlp_in, w_mlp_out=w_mlp_out,
             w_ple=w_ple, ple_norm_g=ple_norm_g, ple_gate_norm_g=ple_gate_norm_g,
             w_ple_gate=w_ple_gate, a_w_qkv=a_w_qkv, a_q_norm_g=a_q_norm_g, a_k_norm_g=a_k_norm_g,
             a_w_o=a_w_o, b_w_dqkv=b_w_dqkv, b_cq_norm_g=b_cq_norm_g, b_ckv_norm_g=b_ckv_norm_g,
             b_w_uq=b_w_uq, b_w_ukv=b_w_ukv, b_q_norm_g=b_q_norm_g, b_k_norm_g=b_k_norm_g,
             b_w_o=b_w_o, c_w_qkv=c_w_qkv, c_q_norm_g=c_q_norm_g, c_k_norm_g=c_k_norm_g, c_w_o=c_w_o)
    pw = _prep_weights(w)
    return (_layer_stack(x_prompt, p_prompt, w, pw), _layer_stack(x_sample, p_sample, w, pw))
```

```python
import functools
import math

import numpy as np
import jax
import jax.numpy as jnp
from jax import lax
from jax.experimental import pallas as pl
from jax.experimental.pallas import tpu as pltpu

F32 = jnp.float32
BF16 = jnp.bfloat16

D_MODEL = 2048
DEPTH = 4
N_MIXERS = 3
PLE_DIM = 256
GRID_W = 64
EPS = 1e-6
D_FF = 4 * D_MODEL

A_HEADS = 16
A_KV_HEADS = 4
A_HEAD_DIM = 128
A_ROPE_THETA = 10000.0

B_HEADS = 16
B_Q_RANK = 512
B_KV_RANK = 512
B_NOPE_DIM = 128
B_ROPE_DIM = 64
B_V_DIM = 128
B_QK_DIM = B_NOPE_DIM + B_ROPE_DIM
B_ROPE_THETA = 10000.0
B_HEAD_PAD = 256
B_LAT_PAD = 1152

C_PAIRS = ((128, 1), (512, 4), (2048, 16))
C_GROUPS = 3
C_HEADS = 8
C_HEAD_DIM = 128
C_ROPE_DIM = 32
C_ROPE_THETA = 500000.0

LANES = 128
DEN_ROWS = 16
LOG2E = math.log2(math.e)
MASK_BIAS = -1e30
VMEM_LIMIT = 56 * 1024 * 1024


def _params(n_axes):
    return pltpu.CompilerParams(dimension_semantics=("arbitrary",) * n_axes,
                                vmem_limit_bytes=VMEM_LIMIT)


def _rms_to_bf16(x_ref, g_ref, xn_ref, rows=128):
    tm = x_ref.shape[0]

    def chunk(c, carry):
        r0 = pl.multiple_of(c * rows, rows)
        xf = x_ref[pl.ds(r0, rows), :]
        ms = jnp.mean(xf * xf, axis=-1, keepdims=True)
        xn_ref[pl.ds(r0, rows), :] = (xf * lax.rsqrt(ms + EPS) * g_ref[...]).astype(BF16)
        return carry

    lax.fori_loop(0, tm // rows, chunk, 0)


def _rope(y, cos, sin):
    return y * cos + pltpu.roll(y, LANES // 2, 1) * sin


def _fused_matmul(x, w, *, tm, tn, chunk, gain=None, x_kblock=0, extras=(), epilogue=None,
                  n_epilogue_cols=None, out_tile=None, out_dtype=BF16, n_transposed_cols=0):
    t_rows = x.shape[0]
    k_dim, n_dim = w.shape
    nj = n_dim // tn
    ow = out_tile or tn
    norm = gain is not None
    n_extra = len(extras)
    n_epi = 0 if epilogue is None else (n_dim if n_epilogue_cols is None else n_epilogue_cols)
    assert nj == 1 or n_epi % tn == 0
    assert n_epi == n_dim or ow == tn
    n_t = n_transposed_cols
    n_main = n_dim - n_t
    assert n_t == 0 or (nj == 1 and n_t % chunk == 0 and n_epi <= n_main and ow == tn)

    def body(*refs):
        x_ref, w_ref = refs[0], refs[1]
        pos = 2
        if norm:
            g_ref = refs[pos]
            pos += 1
        extra_refs = refs[pos:pos + n_extra]
        pos += n_extra
        o_ref = t_ref = None
        if n_main:
            o_ref = refs[pos]
            pos += 1
        if n_t:
            t_ref = refs[pos]
            pos += 1
        j = pl.program_id(1)
        if norm:
            lhs_ref = refs[pos]

            @pl.when(j == 0)
            def _():
                _rms_to_bf16(x_ref, g_ref, lhs_ref)
        else:
            lhs_ref = x_ref

        def run(with_epilogue):
            for c0 in range(0, tn, chunk):
                acc = jnp.dot(lhs_ref[...], w_ref[:, c0:c0 + chunk], preferred_element_type=F32)
                if c0 >= n_main:
                    t_ref[c0 - n_main:c0 - n_main + chunk, :] = acc.T.astype(BF16)
                elif with_epilogue(c0):
                    epilogue(acc, c0, extra_refs, o_ref)
                else:
                    o_ref[:, c0:c0 + chunk] = acc.astype(o_ref.dtype)

        if nj == 1:
            run(lambda c0: c0 < n_epi)
        elif n_epi in (0, n_dim):
            run(lambda c0: n_epi > 0)
        else:
            @pl.when(j < n_epi // tn)
            def _():
                run(lambda c0: True)

            @pl.when(j >= n_epi // tn)
            def _():
                run(lambda c0: False)

    in_specs = [pl.BlockSpec((tm, k_dim), lambda i, j: (i, x_kblock)),
                pl.BlockSpec((k_dim, tn), lambda i, j: (0, j))]
    args = [x, w]
    if norm:
        in_specs.append(pl.BlockSpec((1, k_dim), lambda i, j: (0, 0)))
        args.append(gain)
    for arr, spec in extras:
        in_specs.append(spec)
        args.append(arr)
    scratch = [pltpu.VMEM((tm, k_dim), BF16)] if norm else []
    out_specs, out_shape = [], []
    if n_main:
        main_w = ow if n_t == 0 else n_main
        out_specs.append(pl.BlockSpec((tm, main_w), lambda i, j: (i, j)))
        out_shape.append(jax.ShapeDtypeStruct((t_rows, nj * main_w), out_dtype))
    if n_t:
        out_specs.append(pl.BlockSpec((n_t, tm), lambda i, j: (0, i)))
        out_shape.append(jax.ShapeDtypeStruct((n_t, t_rows), BF16))
    if len(out_specs) == 1:
        out_specs, out_shape = out_specs[0], out_shape[0]
    return pl.pallas_call(
        body,
        grid=(t_rows // tm, nj),
        in_specs=in_specs,
        out_specs=out_specs,
        out_shape=out_shape,
        scratch_shapes=scratch,
        compiler_params=_params(2),
    )(*args)


def _residual_epilogue(acc, c0, extra_refs, o_ref):
    (res_ref,) = extra_refs
    cols = slice(c0, c0 + acc.shape[1])
    o_ref[:, cols] = res_ref[:, cols] + acc


def _out_proj_residual(o, w_o, x, *, tm=512):
    n_dim = w_o.shape[1]
    extras = [(x, pl.BlockSpec((tm, n_dim), lambda i, j: (i, 0)))]
    return _fused_matmul(o, w_o, tm=tm, tn=n_dim, chunk=512, extras=extras,
                         epilogue=_residual_epilogue, out_dtype=F32)


def _lane_sum(v):
    hi = v.astype(BF16)
    lo = (v - hi.astype(F32)).astype(BF16)
    ones = jnp.ones((2 * LANES, LANES), BF16)
    return jnp.dot(jnp.concatenate([hi, lo], axis=1), ones, preferred_element_type=F32)


def _head_norm_rope(c, gain, cos, sin):
    ms = _lane_sum(c * c) * (1.0 / LANES)
    return _rope(c * lax.rsqrt(ms + EPS) * gain, cos, sin)


def _head_norm_rope_epilogue(acc, c0, extra_refs, o_ref):
    g_ref, cos_ref, sin_ref = extra_refs
    for h in range(acc.shape[1] // LANES):
        cols = slice(c0 + h * LANES, c0 + (h + 1) * LANES)
        y = _head_norm_rope(acc[:, h * LANES:(h + 1) * LANES], g_ref[:, cols],
                            cos_ref[...], sin_ref[...])
        o_ref[:, cols] = y.astype(o_ref.dtype)


def _qkv_proj(x, w, norm_g, head_g, cos, sin, *, seq, n_normed_cols, tm, tn, n_transposed_cols=0):
    n_pos_blocks = seq // tm
    extras = [
        (head_g, pl.BlockSpec((1, tn), lambda i, j: (0, j))),
        (cos, pl.BlockSpec((tm, LANES), lambda i, j: (i % n_pos_blocks, 0))),
        (sin, pl.BlockSpec((tm, LANES), lambda i, j: (i % n_pos_blocks, 0))),
    ]
    return _fused_matmul(x, w, tm=tm, tn=tn, chunk=512, gain=norm_g, extras=extras,
                         epilogue=_head_norm_rope_epilogue, n_epilogue_cols=n_normed_cols,
                         n_transposed_cols=n_transposed_cols)


def _mla_q_epilogue(acc, c0, extra_refs, o_ref):
    g_ref, cos_ref, sin_ref = extra_refs
    for h in range(acc.shape[1] // B_HEAD_PAD):
        c_lo = acc[:, h * B_HEAD_PAD:h * B_HEAD_PAD + LANES]
        c_hi = acc[:, h * B_HEAD_PAD + LANES:(h + 1) * B_HEAD_PAD]
        ss = _lane_sum(c_lo * c_lo + c_hi * c_hi) * (1.0 / B_QK_DIM)
        r = lax.rsqrt(ss + EPS)
        base = c0 + h * B_HEAD_PAD
        o_ref[:, base:base + LANES] = (c_lo * r * g_ref[:, :LANES]).astype(o_ref.dtype)
        y_hi = c_hi * r * g_ref[:, LANES:]
        o_ref[:, base + LANES:base + B_HEAD_PAD] = _rope(
            y_hi, cos_ref[...], sin_ref[...]).astype(o_ref.dtype)


def _mla_k_epilogue(acc, c0, extra_refs, o_ref):
    kr_ref, g_ref, cos_ref, sin_ref = extra_refs
    kr = kr_ref[...]
    kr_sq = kr * kr
    for h in range(acc.shape[1] // LANES):
        c = acc[:, h * LANES:(h + 1) * LANES]
        ss = _lane_sum(c * c + kr_sq) * (1.0 / B_QK_DIM)
        r = lax.rsqrt(ss + EPS)
        base = (c0 // LANES + h) * B_HEAD_PAD
        o_ref[:, base:base + LANES] = (c * r * g_ref[:, :LANES]).astype(o_ref.dtype)
        y_hi = kr * r * g_ref[:, LANES:]
        o_ref[:, base + LANES:base + B_HEAD_PAD] = _rope(
            y_hi, cos_ref[...], sin_ref[...]).astype(o_ref.dtype)


def _dense_attention(q_arr, k_arr, vt_arr, *, batch, seq, n_groups, heads, kv_heads, dq,
                     q_cb0, k_cb0, tq=1024, tk=1024):
    nq, nk = seq // tq, seq // tk
    dv = LANES

    def body(q_ref, k_ref, vt_ref, o_ref, qt_ref, m_ref, acc_ref, s0_ref, s1_ref, mc0_ref, mc1_ref):
        ki = pl.program_id(3)
        s_refs, mc_refs = (s0_ref, s1_ref), (mc0_ref, mc1_ref)

        def score_head(slot, g):
            kv = g * kv_heads // heads
            s = jnp.dot(k_ref[:, kv * dq:(kv + 1) * dq], qt_ref[g],
                        preferred_element_type=F32)
            s_refs[slot][g] = s
            mc_refs[slot][g] = jnp.max(s, axis=0, keepdims=True)

        def value_head(slot, g):
            kv = g * kv_heads // heads
            m_prev = m_ref[g]
            m_new = jnp.maximum(m_prev, mc_refs[slot][g])
            alpha = jnp.exp2(m_prev - m_new)
            p = jnp.exp2((s_refs[slot][g] - m_new).astype(BF16))
            ones_rows = jnp.ones((DEN_ROWS, tk), BF16)
            vt_ext = jnp.concatenate([vt_ref[kv * dv:(kv + 1) * dv, :], ones_rows], axis=0)
            pv = jnp.dot(vt_ext, p, preferred_element_type=F32)
            acc_ref[g] = acc_ref[g] * alpha + pv
            m_ref[g] = m_new

        @pl.when(ki == 0)
        def _():
            for g in range(heads):
                qg = q_ref[:, g * dq:(g + 1) * dq].astype(F32)
                qt_ref[g] = qg.T.astype(BF16)
            m_ref[...] = jnp.full(m_ref.shape, -jnp.inf, F32)
            acc_ref[...] = jnp.zeros(acc_ref.shape, F32)
            for g in range(heads):
                score_head(0, g)

        for parity in (0, 1):
            @pl.when((ki > 0) & (ki < nk) & (ki % 2 == parity))
            def _():
                for g in range(heads):
                    value_head(1 - parity, g)
                    score_head(parity, g)

        @pl.when(ki == nk)
        def _():
            for g in range(heads):
                value_head((nk - 1) % 2, g)
            for g in range(heads):
                o = acc_ref[g, :dv, :] / acc_ref[g, dv:dv + 1, :]
                o_ref[:, g * dv:(g + 1) * dv] = o.T.astype(o_ref.dtype)

    t_rows = batch * seq
    return pl.pallas_call(
        body,
        grid=(batch, n_groups, nq, nk + 1),
        in_specs=[
            pl.BlockSpec((tq, heads * dq), lambda b, g, qi, ki: (b * nq + qi, q_cb0 + g)),
            pl.BlockSpec((tk, kv_heads * dq),
                         lambda b, g, qi, ki: (b * nk + jnp.minimum(ki, nk - 1), k_cb0 + g)),
            pl.BlockSpec((kv_heads * dv, tk),
                         lambda b, g, qi, ki: (g, b * nk + jnp.maximum(ki - 1, 0))),
        ],
        out_specs=pl.BlockSpec((tq, heads * dv), lambda b, g, qi, ki: (b * nq + qi, g)),
        out_shape=jax.ShapeDtypeStruct((t_rows, n_groups * heads * dv), BF16),
        scratch_shapes=[
            pltpu.VMEM((heads, dq, tq), BF16),
            pltpu.VMEM((heads, 1, tq), F32),
            pltpu.VMEM((heads, dv + DEN_ROWS, tq), F32),
            pltpu.VMEM((heads, tk, tq), F32),
            pltpu.VMEM((heads, tk, tq), F32),
            pltpu.VMEM((heads, 1, tq), F32),
            pltpu.VMEM((heads, 1, tq), F32),
        ],
        compiler_params=_params(4),
    )(q_arr, k_arr, vt_arr)


C_TL = 512
C_REACH = 64
assert all(window // (2 * dil) == C_REACH for window, dil in C_PAIRS)
assert tuple(dil for _, dil in C_PAIRS) == (1, 4, 16)


def _band_bias(tl):
    i = np.arange(tl)[None, :]
    jh = np.arange(C_REACH)[:, None]
    jc = np.arange(tl)[:, None]

    def as_bias(ok):
        return jnp.asarray(np.where(ok, 0.0, MASK_BIAS), F32)

    return as_bias(jh >= i), as_bias(np.abs(jc - i) <= C_REACH), as_bias(i >= tl - C_REACH + jh)


def _dilated_qkv_proj(x, w, norm_g, head_g, cos, sin, *, batch, seq, tm=1024, chunk=512):
    t_rows, k_dim = x.shape
    hw = C_HEADS * C_HEAD_DIM
    n_steps = 3 * C_GROUPS
    blocks_per_seq = seq // tm
    dils = [dil for _, dil in C_PAIRS]
    assert all(tm % dil == 0 and (tm // dil) % 16 == 0 for dil in dils)

    def body(x_ref, w_ref, g_ref, hg_ref, cos_ref, sin_ref, o0_ref, o1_ref, o2_ref, xn_ref, res_ref):
        j = pl.program_id(1)

        @pl.when(j == 0)
        def _():
            _rms_to_bf16(x_ref, g_ref, xn_ref)

        def step(o_ref, dil, with_epilogue):
            for c0 in range(0, hw, chunk):
                acc = jnp.dot(xn_ref[...], w_ref[:, c0:c0 + chunk], preferred_element_type=F32)
                for h in range(chunk // LANES):
                    c = acc[:, h * LANES:(h + 1) * LANES]
                    head = c0 // LANES + h
                    cols = slice(head * LANES, (head + 1) * LANES)
                    if with_epilogue:
                        c = _head_norm_rope(c, hg_ref[:, cols], cos_ref[...], sin_ref[...])
                    if dil == 1:
                        o_ref[0, :, cols] = c.astype(BF16)
                        continue
                    res_ref[head] = c
                    for r in range(dil):
                        o_ref[r, :, cols] = res_ref[
                            head, pl.ds(r, tm // dil, stride=dil), :].astype(BF16)

        for g, (o_ref, dil) in enumerate(zip((o0_ref, o1_ref, o2_ref), dils)):
            @pl.when((j // 3 == g) & (j % 3 < 2))
            def _():
                step(o_ref, dil, True)

            @pl.when((j // 3 == g) & (j % 3 == 2))
            def _():
                step(o_ref, dil, False)

    def out_spec(g, dil):
        return pl.BlockSpec(
            (dil, tm // dil, hw),
            lambda i, j: (i // blocks_per_seq, i % blocks_per_seq, jnp.clip(j - 3 * g, 0, 2)))

    return pl.pallas_call(
        body,
        grid=(t_rows // tm, n_steps),
        in_specs=[
            pl.BlockSpec((tm, k_dim), lambda i, j: (i, 0)),
            pl.BlockSpec((k_dim, hw), lambda i, j: (0, j)),
            pl.BlockSpec((1, k_dim), lambda i, j: (0, 0)),
            pl.BlockSpec((1, hw), lambda i, j: (0, j)),
            pl.BlockSpec((tm, LANES), lambda i, j: (i % blocks_per_seq, 0)),
            pl.BlockSpec((tm, LANES), lambda i, j: (i % blocks_per_seq, 0)),
        ],
        out_specs=[out_spec(g, dil) for g, dil in enumerate(dils)],
        out_shape=[jax.ShapeDtypeStruct((batch * dil, seq // dil, 3 * hw), BF16) for dil in dils],
        scratch_shapes=[pltpu.VMEM((tm, k_dim), BF16), pltpu.VMEM((C_HEADS, tm, LANES), F32)],
        compiler_params=_params(2),
    )(x, w, norm_g, head_g, cos, sin)


def _banded_attention(x, *, seq_len, tl=C_TL):
    hw = C_HEADS * C_HEAD_DIM
    dh = C_HEAD_DIM
    n_rows = x.shape[0]
    halo_per_block = tl // C_REACH
    n_halo_blocks = n_rows // C_REACH
    nb_seq = seq_len // tl
    assert seq_len % tl == 0
    bias_prev, bias_cur, bias_next = _band_bias(tl)
    nt_dims = (((1,), (1,)), ((), ()))
    tn_dims = (((0,), (0,)), ((), ()))

    def body(q_ref, kp_ref, kc_ref, kn_ref, vp_ref, vc_ref, vn_ref, bp_ref, bc_ref, bn_ref,
             o_ref, lse_ref):
        pos = pl.program_id(0) % nb_seq
        b_prev = bp_ref[...] + jnp.where(pos == 0, MASK_BIAS, 0.0)
        b_next = bn_ref[...] + jnp.where(pos == nb_seq - 1, MASK_BIAS, 0.0)
        b_cur = bc_ref[...]
        lse_rows = []
        for h in range(C_HEADS):
            sl = slice(h * dh, (h + 1) * dh)
            q = q_ref[:, sl]
            s_p = lax.dot_general(kp_ref[:, sl], q, nt_dims, preferred_element_type=F32) + b_prev
            s_c = lax.dot_general(kc_ref[:, sl], q, nt_dims, preferred_element_type=F32) + b_cur
            s_n = lax.dot_general(kn_ref[:, sl], q, nt_dims, preferred_element_type=F32) + b_next
            m = jnp.maximum(jnp.max(s_c, axis=0, keepdims=True),
                            jnp.maximum(jnp.max(s_p, axis=0, keepdims=True),
                                        jnp.max(s_n, axis=0, keepdims=True)))
            p_p = jnp.exp2(s_p - m)
            p_c = jnp.exp2(s_c - m)
            p_n = jnp.exp2(s_n - m)
            l = (jnp.sum(p_c, axis=0, keepdims=True) + jnp.sum(p_p, axis=0, keepdims=True)
                 + jnp.sum(p_n, axis=0, keepdims=True))
            ot = (lax.dot_general(vc_ref[:, sl], p_c.astype(BF16), tn_dims, preferred_element_type=F32)
                  + lax.dot_general(vp_ref[:, sl], p_p.astype(BF16), tn_dims, preferred_element_type=F32)
                  + lax.dot_general(vn_ref[:, sl], p_n.astype(BF16), tn_dims, preferred_element_type=F32))
            o_ref[:, sl] = (ot / l).T
            lse_rows.append((m + jnp.log2(l)) * (1.0 / LOG2E))
        lse_t = jnp.concatenate(lse_rows + [jnp.zeros((LANES - C_HEADS, tl), F32)], axis=0)
        lse_ref[...] = lse_t.T

    def prev_halo(blk):
        return jnp.maximum(blk * halo_per_block - 1, 0)

    def next_halo(blk):
        return jnp.minimum((blk + 1) * halo_per_block, n_halo_blocks - 1)

    const = lambda blk: (0, 0)
    return pl.pallas_call(
        body,
        grid=(n_rows // tl,),
        in_specs=[
            pl.BlockSpec((tl, hw), lambda blk: (blk, 0)),
            pl.BlockSpec((C_REACH, hw), lambda blk: (prev_halo(blk), 1)),
            pl.BlockSpec((tl, hw), lambda blk: (blk, 1)),
            pl.BlockSpec((C_REACH, hw), lambda blk: (next_halo(blk), 1)),
            pl.BlockSpec((C_REACH, hw), lambda blk: (prev_halo(blk), 2)),
            pl.BlockSpec((tl, hw), lambda blk: (blk, 2)),
            pl.BlockSpec((C_REACH, hw), lambda blk: (next_halo(blk), 2)),
            pl.BlockSpec((C_REACH, tl), const),
            pl.BlockSpec((tl, tl), const),
            pl.BlockSpec((C_REACH, tl), const),
        ],
        out_specs=[pl.BlockSpec((tl, hw), lambda blk: (blk, 0)),
                   pl.BlockSpec((tl, LANES), lambda blk: (blk, 0))],
        out_shape=[jax.ShapeDtypeStruct((n_rows, hw), F32),
                   jax.ShapeDtypeStruct((n_rows, LANES), F32)],
        compiler_params=_params(1),
    )(x, x, x, x, x, x, x, bias_prev, bias_cur, bias_next)


def _merge_out_proj_residual(os, lses, w_o, x, *, batch, seq, tm=512, tn=512):
    t_rows = x.shape[0]
    hw, n_dim = w_o.shape
    dh = C_HEAD_DIM
    dils = [dil for _, dil in C_PAIRS]
    blocks_per_seq = seq // tm
    assert all(tm % dil == 0 and (tm // dil) % 8 == 0 for dil in dils)

    def body(o0_ref, o1_ref, o2_ref, l0_ref, l1_ref, l2_ref, w_ref, res_ref, out_ref,
             ot_ref, lt_ref, mg_ref):
        @pl.when(pl.program_id(1) == 0)
        def _():
            for g, (o_ref, l_ref, dil) in enumerate(zip((o0_ref, o1_ref, o2_ref),
                                                        (l0_ref, l1_ref, l2_ref), dils)):
                for r in range(dil):
                    rows = pl.ds(r, tm // dil, stride=dil)
                    for h in range(C_HEADS):
                        ot_ref[g, h, rows, :] = o_ref[r, :, h * dh:(h + 1) * dh]
                    lt_ref[g, rows, :] = l_ref[r]
            lse = [lt_ref[g] for g in range(C_GROUPS)]
            mx = jnp.maximum(lse[0], jnp.maximum(lse[1], lse[2]))
            e = [jnp.exp(v - mx) for v in lse]
            den = e[0] + e[1] + e[2]
            wts = [v / den for v in e]
            for h in range(C_HEADS):
                sl = slice(h * dh, (h + 1) * dh)
                mg = (wts[0][:, h:h + 1] * ot_ref[0, h] + wts[1][:, h:h + 1] * ot_ref[1, h]
                      + wts[2][:, h:h + 1] * ot_ref[2, h])
                mg_ref[:, sl] = mg.astype(BF16)

        out_ref[...] = res_ref[...] + jnp.dot(mg_ref[...], w_ref[...], preferred_element_type=F32)

    def group_spec(dil, width):
        return pl.BlockSpec((dil, tm // dil, width),
                            lambda i, j: (i // blocks_per_seq, i % blocks_per_seq, 0))

    args = ([o.reshape(batch * dil, seq // dil, hw) for o, dil in zip(os, dils)]
            + [l.reshape(batch * dil, seq // dil, LANES) for l, dil in zip(lses, dils)])
    return pl.pallas_call(
        body,
        grid=(t_rows // tm, n_dim // tn),
        in_specs=([group_spec(dil, hw) for dil in dils] + [group_spec(dil, LANES) for dil in dils]
                  + [pl.BlockSpec((hw, tn), lambda i, j: (0, j)),
                     pl.BlockSpec((tm, tn), lambda i, j: (i, j))]),
        out_specs=pl.BlockSpec((tm, tn), lambda i, j: (i, j)),
        out_shape=jax.ShapeDtypeStruct((t_rows, n_dim), F32),
        scratch_shapes=[pltpu.VMEM((C_GROUPS, C_HEADS, tm, dh), F32),
                        pltpu.VMEM((C_GROUPS, tm, LANES), F32),
                        pltpu.VMEM((tm, hw), BF16)],
        compiler_params=_params(2),
    )(*args, w_o, x)


def _mlp_residual(x, gain, w_in, w_out, *, tm=1024, tf=512):
    t_rows, d = x.shape
    nf = w_in.shape[1] // tf

    def body(x_ref, g_ref, wi_ref, wo_ref, o_ref, xn_ref):
        @pl.when(pl.program_id(1) == 0)
        def _():
            _rms_to_bf16(x_ref, g_ref, xn_ref)
            o_ref[...] = x_ref[...]

        h = jnp.dot(xn_ref[...], wi_ref[...], preferred_element_type=F32)
        h = jnp.square(jnp.maximum(h, 0.0)).astype(BF16)
        o_ref[...] += jnp.dot(h, wo_ref[...], preferred_element_type=F32)

    return pl.pallas_call(
        body,
        grid=(t_rows // tm, nf),
        in_specs=[
            pl.BlockSpec((tm, d), lambda i, f: (i, 0)),
            pl.BlockSpec((1, d), lambda i, f: (0, 0)),
            pl.BlockSpec((d, tf), lambda i, f: (0, f)),
            pl.BlockSpec((tf, d), lambda i, f: (f, 0)),
        ],
        out_specs=pl.BlockSpec((tm, d), lambda i, f: (i, 0)),
        out_shape=jax.ShapeDtypeStruct((t_rows, d), F32),
        scratch_shapes=[pltpu.VMEM((tm, d), BF16)],
        compiler_params=_params(2),
    )(x, gain, w_in, w_out)


def _ple_residual(x, p, gate_g, w_gate, w_ple, ple_g, *, tm=512):
    t_rows, d = x.shape
    pd = p.shape[1]

    def body(x_ref, p_ref, gg_ref, wg_ref, wp_ref, pg_ref, o_ref, xn_ref):
        _rms_to_bf16(x_ref, gg_ref, xn_ref)
        e = jnp.dot(p_ref[...].astype(BF16), wp_ref[...], preferred_element_type=F32)
        ms = jnp.mean(e * e, axis=-1, keepdims=True)
        e = e * lax.rsqrt(ms + EPS) * pg_ref[...]
        z = jnp.dot(xn_ref[...], wg_ref[...], preferred_element_type=F32)
        o_ref[...] = x_ref[...] + jax.nn.sigmoid(z) * e

    const = lambda i: (0, 0)
    return pl.pallas_call(
        body,
        grid=(t_rows // tm,),
        in_specs=[
            pl.BlockSpec((tm, d), lambda i: (i, 0)),
            pl.BlockSpec((tm, pd), lambda i: (i, 0)),
            pl.BlockSpec((1, d), const),
            pl.BlockSpec((d, d), const, pipeline_mode=pl.Buffered(1)),
            pl.BlockSpec((pd, d), const, pipeline_mode=pl.Buffered(1)),
            pl.BlockSpec((1, d), const),
        ],
        out_specs=pl.BlockSpec((tm, d), lambda i: (i, 0)),
        out_shape=jax.ShapeDtypeStruct((t_rows, d), F32),
        scratch_shapes=[pltpu.VMEM((tm, d), BF16)],
        compiler_params=_params(1),
    )(x, p, gate_g, w_gate, w_ple, ple_g)


def _rope_cos_sin(pos, dim, theta):
    inv_freq = theta ** (-jnp.arange(0, dim, 2, dtype=F32) / dim)
    ang = pos.astype(F32)[:, None] * inv_freq[None, :]
    return jnp.cos(ang), jnp.sin(ang)


def _pair_perm(first, second, rest):
    fill = LANES // 2 - len(first)
    perm = list(first) + list(rest[:fill]) + list(second) + list(rest[fill:])
    assert sorted(perm) == list(range(LANES))
    return np.asarray(perm)


A_PERM = _pair_perm(list(range(0, 32)) + list(range(64, 96)),
                    list(range(32, 64)) + list(range(96, 128)), [])
B_PERM = _pair_perm(range(0, 32), range(32, 64), list(range(64, 128)))
C_PERM = _pair_perm(range(0, 16), range(16, 32), list(range(32, 128)))


def _permute_heads(w_cols, n_heads, perm):
    lead = w_cols.shape[:-1]
    n_cols = n_heads * LANES
    heads = w_cols[..., :n_cols].reshape(lead + (n_heads, LANES))[..., perm]
    return jnp.concatenate([heads.reshape(lead + (n_cols,)), w_cols[..., n_cols:]], axis=-1)


def _axial_tables(seq):
    t = jnp.arange(seq)
    half = A_HEAD_DIM // 2
    cr, sr = _rope_cos_sin(t // GRID_W, half, A_ROPE_THETA)
    cc, sc = _rope_cos_sin(t % GRID_W, half, A_ROPE_THETA)
    return (jnp.concatenate([cr, cc, cr, cc], axis=-1),
            jnp.concatenate([-sr, -sc, sr, sc], axis=-1))


def _partial_tables(seq, rope_dim, theta):
    c, s = _rope_cos_sin(jnp.arange(seq), rope_dim, theta)
    fill = LANES // 2 - rope_dim // 2
    one, zero = jnp.ones((seq, fill), F32), jnp.zeros((seq, fill), F32)
    return (jnp.concatenate([c, one, c, one], axis=-1),
            jnp.concatenate([-s, zero, s, zero], axis=-1))


def _row(v):
    return v.reshape(1, -1).astype(F32)


def _prep_weights(w):
    out = {}

    def per_layer(stack):
        return [stack[i].astype(BF16) for i in range(stack.shape[0])]

    for name in ("w_mlp_in", "w_mlp_out", "w_ple", "w_ple_gate", "a_w_o", "b_w_o", "c_w_o"):
        out[name] = per_layer(w[name])
    out["a_w_qkv"] = _permute_heads(w["a_w_qkv"], A_HEADS + A_KV_HEADS, A_PERM).astype(BF16)
    out["a_head_g"] = jnp.concatenate(
        [jnp.tile(w["a_q_norm_g"][:, A_PERM] * (A_HEAD_DIM ** -0.5 * LOG2E), (1, A_HEADS)),
         jnp.tile(w["a_k_norm_g"][:, A_PERM], (1, A_KV_HEADS)),
         jnp.ones((w["a_q_norm_g"].shape[0], A_KV_HEADS * A_HEAD_DIM), F32)], axis=-1)

    n_b = w["b_w_dqkv"].shape[0]
    lat_w = B_Q_RANK + B_KV_RANK + B_ROPE_DIM
    n_lat_groups = B_LAT_PAD // LANES

    def permute_last_group(a, n_groups):
        lead = a.shape[:-1]
        grouped = a.reshape(lead + (n_groups, LANES))
        last = grouped[..., n_groups - 1:, :][..., B_PERM]
        return jnp.concatenate([grouped[..., :n_groups - 1, :], last], axis=-2).reshape(a.shape)

    dqkv = jnp.pad(w["b_w_dqkv"], ((0, 0), (0, 0), (0, B_LAT_PAD - lat_w)))
    out["b_w_dqkv"] = permute_last_group(dqkv, n_lat_groups).astype(BF16)
    uq = w["b_w_uq"].reshape(n_b, B_Q_RANK, B_HEADS, B_QK_DIM)
    uq = jnp.pad(uq, ((0, 0), (0, 0), (0, 0), (0, B_HEAD_PAD - B_QK_DIM)))
    uq = permute_last_group(uq, B_HEAD_PAD // LANES)
    out["b_w_uq"] = uq.reshape(n_b, B_Q_RANK, B_HEADS * B_HEAD_PAD).astype(BF16)
    ukv = w["b_w_ukv"].reshape(n_b, B_KV_RANK, B_HEADS, B_NOPE_DIM + B_V_DIM)
    out["b_w_uk"] = ukv[..., :B_NOPE_DIM].reshape(n_b, B_KV_RANK, B_HEADS * B_NOPE_DIM).astype(BF16)
    out["b_w_uv"] = ukv[..., B_NOPE_DIM:].reshape(n_b, B_KV_RANK, B_HEADS * B_V_DIM).astype(BF16)
    pad_g = ((0, 0), (0, B_HEAD_PAD - B_QK_DIM))
    out["b_q_g"] = permute_last_group(
        jnp.pad(w["b_q_norm_g"] * (B_QK_DIM ** -0.5 * LOG2E), pad_g), B_HEAD_PAD // LANES)
    out["b_k_g"] = permute_last_group(jnp.pad(w["b_k_norm_g"], pad_g), B_HEAD_PAD // LANES)

    n_ch = C_GROUPS * C_HEADS
    hw = C_HEADS * C_HEAD_DIM

    def group_major(a):
        lead = a.shape[:-1]
        a = a.reshape(lead + (3, C_GROUPS, hw))
        return jnp.swapaxes(a, -3, -2).reshape(lead + (3 * C_GROUPS * hw,))

    out["c_w_qkv"] = group_major(_permute_heads(w["c_w_qkv"], 2 * n_ch, C_PERM)).astype(BF16)
    out["c_head_g"] = group_major(jnp.concatenate(
        [jnp.tile(w["c_q_norm_g"][:, C_PERM] * (C_HEAD_DIM ** -0.5 * LOG2E), (1, n_ch)),
         jnp.tile(w["c_k_norm_g"][:, C_PERM], (1, n_ch)),
         jnp.ones((w["c_q_norm_g"].shape[0], n_ch * C_HEAD_DIM), F32)], axis=-1))
    return out


def _mixer_a(x, norm_g, w, pw, j, batch, seq):
    cos, sin = _axial_tables(seq)
    w_qkv = pw["a_w_qkv"][j]
    qk, vt = _qkv_proj(x, w_qkv, norm_g, _row(pw["a_head_g"][j]), cos, sin, seq=seq,
                       n_normed_cols=(A_HEADS + A_KV_HEADS) * A_HEAD_DIM, tm=512,
                       tn=w_qkv.shape[1], n_transposed_cols=A_KV_HEADS * A_HEAD_DIM)
    group = A_HEADS // A_KV_HEADS
    o = _dense_attention(qk, qk, vt, batch=batch, seq=seq, n_groups=A_KV_HEADS,
                         heads=group, kv_heads=1, dq=A_HEAD_DIM,
                         q_cb0=0, k_cb0=A_HEADS)
    return _out_proj_residual(o, pw["a_w_o"][j], x)


def _mixer_b(x, norm_g, w, pw, j, batch, seq):
    cos, sin = _partial_tables(seq, B_ROPE_DIM, B_ROPE_THETA)
    tm = 512
    n_pos_blocks = seq // tm
    lat = _fused_matmul(x, pw["b_w_dqkv"][j], tm=tm, tn=B_LAT_PAD, chunk=B_LAT_PAD // 3,
                        gain=norm_g, out_dtype=F32)

    def pos_spec():
        return pl.BlockSpec((tm, LANES), lambda i, jj: (i % n_pos_blocks, 0))

    head_g_spec = pl.BlockSpec((1, B_HEAD_PAD), lambda i, jj: (0, 0))
    q = _fused_matmul(lat, pw["b_w_uq"][j], tm=tm, tn=B_HEADS * B_HEAD_PAD, chunk=512,
                      gain=_row(w["b_cq_norm_g"][j]), x_kblock=0,
                      extras=[(_row(pw["b_q_g"][j]), head_g_spec), (cos, pos_spec()), (sin, pos_spec())],
                      epilogue=_mla_q_epilogue)
    ckv_g = _row(w["b_ckv_norm_g"][j])
    k_rope_block = (B_Q_RANK + B_KV_RANK) // LANES
    k = _fused_matmul(lat, pw["b_w_uk"][j], tm=tm, tn=B_HEADS * B_NOPE_DIM, chunk=512,
                      gain=ckv_g, x_kblock=1,
                      extras=[(lat, pl.BlockSpec((tm, LANES), lambda i, jj: (i, k_rope_block))),
                              (_row(pw["b_k_g"][j]), head_g_spec), (cos, pos_spec()), (sin, pos_spec())],
                      epilogue=_mla_k_epilogue, out_tile=B_HEADS * B_HEAD_PAD)
    vt = _fused_matmul(lat, pw["b_w_uv"][j], tm=tm, tn=B_HEADS * B_V_DIM, chunk=512,
                       gain=ckv_g, x_kblock=1, n_transposed_cols=B_HEADS * B_V_DIM)
    heads = 4
    o = _dense_attention(q, k, vt, batch=batch, seq=seq, n_groups=B_HEADS // heads,
                         heads=heads, kv_heads=heads, dq=B_HEAD_PAD,
                         q_cb0=0, k_cb0=0)
    return _out_proj_residual(o, pw["b_w_o"][j], x)


def _mixer_c(x, norm_g, w, pw, j, batch, seq):
    cos, sin = _partial_tables(seq, C_ROPE_DIM, C_ROPE_THETA)
    n_ch = C_GROUPS * C_HEADS
    by_group = _dilated_qkv_proj(x, pw["c_w_qkv"][j], norm_g, _row(pw["c_head_g"][j]), cos, sin,
                                 batch=batch, seq=seq)
    os, lses = [], []
    for xg, (_, dil) in zip(by_group, C_PAIRS):
        o, lse = _banded_attention(xg.reshape(batch * seq, xg.shape[-1]), seq_len=seq // dil,
                                   tl=min(C_TL, seq // dil))
        os.append(o)
        lses.append(lse)
    return _merge_out_proj_residual(os, lses, pw["c_w_o"][j], x, batch=batch, seq=seq)


def _layer_stack(x, p, w, pw):
    batch, seq, d = x.shape
    x = x.reshape(batch * seq, d)
    p = p.reshape(DEPTH, batch * seq, PLE_DIM)
    for i in range(DEPTH):
        kind, j = i % N_MIXERS, i // N_MIXERS
        mixer = (_mixer_a, _mixer_b, _mixer_c)[kind]
        x = mixer(x, _row(w["norm_mix_g"][i]), w, pw, j, batch, seq)
        x = _mlp_residual(x, _row(w["norm_mlp_g"][i]), pw["w_mlp_in"][i], pw["w_mlp_out"][i])
        x = _ple_residual(x, p[i], _row(w["ple_gate_norm_g"][i]), pw["w_ple_gate"][i],
                          pw["w_ple"][i], _row(w["ple_norm_g"][i]))
    return x.reshape(batch, seq, d)


def kernel(x_prompt, x_sample, p_prompt, p_sample, norm_mix_g, norm_mlp_g, w_mlp_in, w_mlp_out, w_ple, ple_norm_g, ple_gate_norm_g, w_ple_gate, a_w_qkv, a_q_norm_g, a_k_norm_g, a_w_o, b_w_dqkv, b_cq_norm_g, b_ckv_norm_g, b_w_uq, b_w_ukv, b_q_norm_g, b_k_norm_g, b_w_o, c_w_qkv, c_q_norm_g, c_k_norm_g, c_w_o):
    w = dict(norm_mix_g=norm_mix_g, norm_mlp_g=norm_mlp_g, w_mlp_in=w_mlp_in, w_mlp_out=w_mlp_out,
             w_ple=w_ple, ple_norm_g=ple_norm_g, ple_gate_norm_g=ple_gate_norm_g,
             w_ple_gate=w_ple_gate, a_w_qkv=a_w_qkv, a_q_norm_g=a_q_norm_g, a_k_norm_g=a_k_norm_g,
             a_w_o=a_w_o, b_w_dqkv=b_w_dqkv, b_cq_norm_g=b_cq_norm_g, b_ckv_norm_g=b_ckv_norm_g,
             b_w_uq=b_w_uq, b_w_ukv=b_w_ukv, b_q_norm_g=b_q_norm_g, b_k_norm_g=b_k_norm_g,
             b_w_o=b_w_o, c_w_qkv=c_w_qkv, c_q_norm_g=c_q_norm_g, c_k_norm_g=c_k_norm_g, c_w_o=c_w_o)
    pw = _prep_weights(w)
    return (_layer_stack(x_prompt, p_prompt, w, pw), _layer_stack(x_sample, p_sample, w, pw))
```

```python
import functools
import math

import numpy as np
import jax
import jax.numpy as jnp
from jax import lax
from jax.experimental import pallas as pl
from jax.experimental.pallas import tpu as pltpu

F32 = jnp.float32
BF16 = jnp.bfloat16

D_MODEL = 2048
DEPTH = 4
N_MIXERS = 3
PLE_DIM = 256
GRID_W = 64
EPS = 1e-6
D_FF = 4 * D_MODEL

A_HEADS = 16
A_KV_HEADS = 4
A_HEAD_DIM = 128
A_ROPE_THETA = 10000.0

B_HEADS = 16
B_Q_RANK = 512
B_KV_RANK = 512
B_NOPE_DIM = 128
B_ROPE_DIM = 64
B_V_DIM = 128
B_QK_DIM = B_NOPE_DIM + B_ROPE_DIM
B_ROPE_THETA = 10000.0
B_HEAD_PAD = 256
B_LAT_PAD = 1152

C_PAIRS = ((128, 1), (512, 4), (2048, 16))
C_GROUPS = 3
C_HEADS = 8
C_HEAD_DIM = 128
C_ROPE_DIM = 32
C_ROPE_THETA = 500000.0

LANES = 128
DEN_ROWS = 16
LOG2E = math.log2(math.e)
MASK_BIAS = -1e30
VMEM_LIMIT = 56 * 1024 * 1024


def _params(n_axes):
    return pltpu.CompilerParams(dimension_semantics=("arbitrary",) * n_axes,
                                vmem_limit_bytes=VMEM_LIMIT)


def _rms_to_bf16(x_ref, g_ref, xn_ref, rows=128):
    tm = x_ref.shape[0]

    def chunk(c, carry):
        r0 = pl.multiple_of(c * rows, rows)
        xf = x_ref[pl.ds(r0, rows), :]
        ms = jnp.mean(xf * xf, axis=-1, keepdims=True)
        xn_ref[pl.ds(r0, rows), :] = (xf * lax.rsqrt(ms + EPS) * g_ref[...]).astype(BF16)
        return carry

    lax.fori_loop(0, tm // rows, chunk, 0)


def _rope(y, cos, sin):
    return y * cos + pltpu.roll(y, LANES // 2, 1) * sin


def _fused_matmul(x, w, *, tm, tn, chunk, gain=None, x_kblock=0, extras=(), epilogue=None,
                  n_epilogue_cols=None, out_tile=None, out_dtype=BF16, n_transposed_cols=0):
    t_rows = x.shape[0]
    k_dim, n_dim = w.shape
    nj = n_dim // tn
    ow = out_tile or tn
    norm = gain is not None
    n_extra = len(extras)
    n_epi = 0 if epilogue is None else (n_dim if n_epilogue_cols is None else n_epilogue_cols)
    assert nj == 1 or n_epi % tn == 0
    assert n_epi == n_dim or ow == tn
    n_t = n_transposed_cols
    n_main = n_dim - n_t
    assert n_t == 0 or (nj == 1 and n_t % chunk == 0 and n_epi <= n_main and ow == tn)

    def body(*refs):
        x_ref, w_ref = refs[0], refs[1]
        pos = 2
        if norm:
            g_ref = refs[pos]
            pos += 1
        extra_refs = refs[pos:pos + n_extra]
        pos += n_extra
        o_ref = t_ref = None
        if n_main:
            o_ref = refs[pos]
            pos += 1
        if n_t:
            t_ref = refs[pos]
            pos += 1
        j = pl.program_id(1)
        if norm:
            lhs_ref = refs[pos]

            @pl.when(j == 0)
            def _():
                _rms_to_bf16(x_ref, g_ref, lhs_ref)
        else:
            lhs_ref = x_ref

        def run(with_epilogue):
            for c0 in range(0, tn, chunk):
                acc = jnp.dot(lhs_ref[...], w_ref[:, c0:c0 + chunk], preferred_element_type=F32)
                if c0 >= n_main:
                    t_ref[c0 - n_main:c0 - n_main + chunk, :] = acc.T.astype(BF16)
                elif with_epilogue(c0):
                    epilogue(acc, c0, extra_refs, o_ref)
                else:
                    o_ref[:, c0:c0 + chunk] = acc.astype(o_ref.dtype)

        if nj == 1:
            run(lambda c0: c0 < n_epi)
        elif n_epi in (0, n_dim):
            run(lambda c0: n_epi > 0)
        else:
            @pl.when(j < n_epi // tn)
            def _():
                run(lambda c0: True)

            @pl.when(j >= n_epi // tn)
            def _():
                run(lambda c0: False)

    in_specs = [pl.BlockSpec((tm, k_dim), lambda i, j: (i, x_kblock)),
                pl.BlockSpec((k_dim, tn), lambda i, j: (0, j))]
    args = [x, w]
    if norm:
        in_specs.append(pl.BlockSpec((1, k_dim), lambda i, j: (0, 0)))
        args.append(gain)
    for arr, spec in extras:
        in_specs.append(spec)
        args.append(arr)
    scratch = [pltpu.VMEM((tm, k_dim), BF16)] if norm else []
    out_specs, out_shape = [], []
    if n_main:
        main_w = ow if n_t == 0 else n_main
        out_specs.append(pl.BlockSpec((tm, main_w), lambda i, j: (i, j)))
        out_shape.append(jax.ShapeDtypeStruct((t_rows, nj * main_w), out_dtype))
    if n_t:
        out_specs.append(pl.BlockSpec((n_t, tm), lambda i, j: (0, i)))
        out_shape.append(jax.ShapeDtypeStruct((n_t, t_rows), BF16))
    if len(out_specs) == 1:
        out_specs, out_shape = out_specs[0], out_shape[0]
    return pl.pallas_call(
        body,
        grid=(t_rows // tm, nj),
        in_specs=in_specs,
        out_specs=out_specs,
        out_shape=out_shape,
        scratch_shapes=scratch,
        compiler_params=_params(2),
    )(*args)


def _residual_epilogue(acc, c0, extra_refs, o_ref):
    (res_ref,) = extra_refs
    cols = slice(c0, c0 + acc.shape[1])
    o_ref[:, cols] = res_ref[:, cols] + acc


def _out_proj_residual(o, w_o, x, *, tm=512):
    n_dim = w_o.shape[1]
    extras = [(x, pl.BlockSpec((tm, n_dim), lambda i, j: (i, 0)))]
    return _fused_matmul(o, w_o, tm=tm, tn=n_dim, chunk=512, extras=extras,
                         epilogue=_residual_epilogue, out_dtype=F32)


def _lane_sum(v):
    hi = v.astype(BF16)
    lo = (v - hi.astype(F32)).astype(BF16)
    ones = jnp.ones((2 * LANES, LANES), BF16)
    return jnp.dot(jnp.concatenate([hi, lo], axis=1), ones, preferred_element_type=F32)


def _head_norm_rope(c, gain, cos, sin):
    ms = _lane_sum(c * c) * (1.0 / LANES)
    return _rope(c * lax.rsqrt(ms + EPS) * gain, cos, sin)


def _head_norm_rope_epilogue(acc, c0, extra_refs, o_ref):
    g_ref, cos_ref, sin_ref = extra_refs
    for h in range(acc.shape[1] // LANES):
        cols = slice(c0 + h * LANES, c0 + (h + 1) * LANES)
        y = _head_norm_rope(acc[:, h * LANES:(h + 1) * LANES], g_ref[:, cols],
                            cos_ref[...], sin_ref[...])
        o_ref[:, cols] = y.astype(o_ref.dtype)


def _qkv_proj(x, w, norm_g, head_g, cos, sin, *, seq, n_normed_cols, tm, tn, n_transposed_cols=0):
    n_pos_blocks = seq // tm
    extras = [
        (head_g, pl.BlockSpec((1, tn), lambda i, j: (0, j))),
        (cos, pl.BlockSpec((tm, LANES), lambda i, j: (i % n_pos_blocks, 0))),
        (sin, pl.BlockSpec((tm, LANES), lambda i, j: (i % n_pos_blocks, 0))),
    ]
    return _fused_matmul(x, w, tm=tm, tn=tn, chunk=512, gain=norm_g, extras=extras,
                         epilogue=_head_norm_rope_epilogue, n_epilogue_cols=n_normed_cols,
                         n_transposed_cols=n_transposed_cols)


def _mla_q_epilogue(acc, c0, extra_refs, o_ref):
    g_ref, cos_ref, sin_ref = extra_refs
    for h in range(acc.shape[1] // B_HEAD_PAD):
        c_lo = acc[:, h * B_HEAD_PAD:h * B_HEAD_PAD + LANES]
        c_hi = acc[:, h * B_HEAD_PAD + LANES:(h + 1) * B_HEAD_PAD]
        ss = _lane_sum(c_lo * c_lo + c_hi * c_hi) * (1.0 / B_QK_DIM)
        r = lax.rsqrt(ss + EPS)
        base = c0 + h * B_HEAD_PAD
        o_ref[:, base:base + LANES] = (c_lo * r * g_ref[:, :LANES]).astype(o_ref.dtype)
        y_hi = c_hi * r * g_ref[:, LANES:]
        o_ref[:, base + LANES:base + B_HEAD_PAD] = _rope(
            y_hi, cos_ref[...], sin_ref[...]).astype(o_ref.dtype)


def _mla_k_epilogue(acc, c0, extra_refs, o_ref):
    kr_ref, g_ref, cos_ref, sin_ref = extra_refs
    kr = kr_ref[...]
    kr_sq = kr * kr
    for h in range(acc.shape[1] // LANES):
        c = acc[:, h * LANES:(h + 1) * LANES]
        ss = _lane_sum(c * c + kr_sq) * (1.0 / B_QK_DIM)
        r = lax.rsqrt(ss + EPS)
        base = (c0 // LANES + h) * B_HEAD_PAD
        o_ref[:, base:base + LANES] = (c * r * g_ref[:, :LANES]).astype(o_ref.dtype)
        y_hi = kr * r * g_ref[:, LANES:]
        o_ref[:, base + LANES:base + B_HEAD_PAD] = _rope(
            y_hi, cos_ref[...], sin_ref[...]).astype(o_ref.dtype)


def _dense_attention(q_arr, k_arr, vt_arr, *, batch, seq, n_groups, heads, kv_heads, dq,
                     q_cb0, k_cb0, tq=1024, tk=1024):
    nq, nk = seq // tq, seq // tk
    dv = LANES

    def body(q_ref, k_ref, vt_ref, o_ref, qt_ref, m_ref, acc_ref, s0_ref, s1_ref, mc0_ref, mc1_ref):
        ki = pl.program_id(3)
        s_refs, mc_refs = (s0_ref, s1_ref), (mc0_ref, mc1_ref)

        def score_head(slot, g):
            kv = g * kv_heads // heads
            s = jnp.dot(k_ref[:, kv * dq:(kv + 1) * dq], qt_ref[g],
                        preferred_element_type=F32)
            s_refs[slot][g] = s
            mc_refs[slot][g] = jnp.max(s, axis=0, keepdims=True)

        def value_head(slot, g):
            kv = g * kv_heads // heads
            m_prev = m_ref[g]
            m_new = jnp.maximum(m_prev, mc_refs[slot][g])
            alpha = jnp.exp2(m_prev - m_new)
            p = jnp.exp2((s_refs[slot][g] - m_new).astype(BF16))
            ones_rows = jnp.ones((DEN_ROWS, tk), BF16)
            vt_ext = jnp.concatenate([vt_ref[kv * dv:(kv + 1) * dv, :], ones_rows], axis=0)
            pv = jnp.dot(vt_ext, p, preferred_element_type=F32)
            acc_ref[g] = acc_ref[g] * alpha + pv
            m_ref[g] = m_new

        @pl.when(ki == 0)
        def _():
            for g in range(heads):
                qg = q_ref[:, g * dq:(g + 1) * dq].astype(F32)
                qt_ref[g] = qg.T.astype(BF16)
            m_ref[...] = jnp.full(m_ref.shape, -jnp.inf, F32)
            acc_ref[...] = jnp.zeros(acc_ref.shape, F32)
            for g in range(heads):
                score_head(0, g)

        for parity in (0, 1):
            @pl.when((ki > 0) & (ki < nk) & (ki % 2 == parity))
            def _():
                for g in range(heads):
                    value_head(1 - parity, g)
                    score_head(parity, g)

        @pl.when(ki == nk)
        def _():
            for g in range(heads):
                value_head((nk - 1) % 2, g)
            for g in range(heads):
                o = acc_ref[g, :dv, :] / acc_ref[g, dv:dv + 1, :]
                o_ref[:, g * dv:(g + 1) * dv] = o.T.astype(o_ref.dtype)

    t_rows = batch * seq
    return pl.pallas_call(
        body,
        grid=(batch, n_groups, nq, nk + 1),
        in_specs=[
            pl.BlockSpec((tq, heads * dq), lambda b, g, qi, ki: (b * nq + qi, q_cb0 + g)),
            pl.BlockSpec((tk, kv_heads * dq),
                         lambda b, g, qi, ki: (b * nk + jnp.minimum(ki, nk - 1), k_cb0 + g)),
            pl.BlockSpec((kv_heads * dv, tk),
                         lambda b, g, qi, ki: (g, b * nk + jnp.maximum(ki - 1, 0))),
        ],
        out_specs=pl.BlockSpec((tq, heads * dv), lambda b, g, qi, ki: (b * nq + qi, g)),
        out_shape=jax.ShapeDtypeStruct((t_rows, n_groups * heads * dv), BF16),
        scratch_shapes=[
            pltpu.VMEM((heads, dq, tq), BF16),
            pltpu.VMEM((heads, 1, tq), F32),
            pltpu.VMEM((heads, dv + DEN_ROWS, tq), F32),
            pltpu.VMEM((heads, tk, tq), F32),
            pltpu.VMEM((heads, tk, tq), F32),
            pltpu.VMEM((heads, 1, tq), F32),
            pltpu.VMEM((heads, 1, tq), F32),
        ],
        compiler_params=_params(4),
    )(q_arr, k_arr, vt_arr)


C_TL = 512
C_REACH = 64
assert all(window // (2 * dil) == C_REACH for window, dil in C_PAIRS)
assert tuple(dil for _, dil in C_PAIRS) == (1, 4, 16)


def _band_bias(tl):
    i = np.arange(tl)[None, :]
    jh = np.arange(C_REACH)[:, None]
    jc = np.arange(tl)[:, None]

    def as_bias(ok):
        return jnp.asarray(np.where(ok, 0.0, MASK_BIAS), F32)

    return as_bias(jh >= i), as_bias(np.abs(jc - i) <= C_REACH), as_bias(i >= tl - C_REACH + jh)


def _dilated_qkv_proj(x, w, norm_g, head_g, cos, sin, *, batch, seq, tm=1024, chunk=512):
    t_rows, k_dim = x.shape
    hw = C_HEADS * C_HEAD_DIM
    n_steps = 3 * C_GROUPS
    blocks_per_seq = seq // tm
    dils = [dil for _, dil in C_PAIRS]
    assert all(tm % dil == 0 and (tm // dil) % 16 == 0 for dil in dils)

    def body(x_ref, w_ref, g_ref, hg_ref, cos_ref, sin_ref, o0_ref, o1_ref, o2_ref, xn_ref, res_ref):
        j = pl.program_id(1)

        @pl.when(j == 0)
        def _():
            _rms_to_bf16(x_ref, g_ref, xn_ref)

        def step(o_ref, dil, with_epilogue):
            for c0 in range(0, hw, chunk):
                acc = jnp.dot(xn_ref[...], w_ref[:, c0:c0 + chunk], preferred_element_type=F32)
                for h in range(chunk // LANES):
                    c = acc[:, h * LANES:(h + 1) * LANES]
                    head = c0 // LANES + h
                    cols = slice(head * LANES, (head + 1) * LANES)
                    if with_epilogue:
                        c = _head_norm_rope(c, hg_ref[:, cols], cos_ref[...], sin_ref[...])
                    if dil == 1:
                        o_ref[0, :, cols] = c.astype(BF16)
                        continue
                    res_ref[head] = c
                    for r in range(dil):
                        o_ref[r, :, cols] = res_ref[
                            head, pl.ds(r, tm // dil, stride=dil), :].astype(BF16)

        for g, (o_ref, dil) in enumerate(zip((o0_ref, o1_ref, o2_ref), dils)):
            @pl.when((j // 3 == g) & (j % 3 < 2))
            def _():
                step(o_ref, dil, True)

            @pl.when((j // 3 == g) & (j % 3 == 2))
            def _():
                step(o_ref, dil, False)

    def out_spec(g, dil):
        return pl.BlockSpec(
            (dil, tm // dil, hw),
            lambda i, j: (i // blocks_per_seq, i % blocks_per_seq, jnp.clip(j - 3 * g, 0, 2)))

    return pl.pallas_call(
        body,
        grid=(t_rows // tm, n_steps),
        in_specs=[
            pl.BlockSpec((tm, k_dim), lambda i, j: (i, 0)),
            pl.BlockSpec((k_dim, hw), lambda i, j: (0, j)),
            pl.BlockSpec((1, k_dim), lambda i, j: (0, 0)),
            pl.BlockSpec((1, hw), lambda i, j: (0, j)),
            pl.BlockSpec((tm, LANES), lambda i, j: (i % blocks_per_seq, 0)),
            pl.BlockSpec((tm, LANES), lambda i, j: (i % blocks_per_seq, 0)),
        ],
        out_specs=[out_spec(g, dil) for g, dil in enumerate(dils)],
        out_shape=[jax.ShapeDtypeStruct((batch * dil, seq // dil, 3 * hw), BF16) for dil in dils],
        scratch_shapes=[pltpu.VMEM((tm, k_dim), BF16), pltpu.VMEM((C_HEADS, tm, LANES), F32)],
        compiler_params=_params(2),
    )(x, w, norm_g, head_g, cos, sin)


def _banded_attention(x, *, seq_len, tl=C_TL):
    hw = C_HEADS * C_HEAD_DIM
    dh = C_HEAD_DIM
    n_rows = x.shape[0]
    halo_per_block = tl // C_REACH
    n_halo_blocks = n_rows // C_REACH
    nb_seq = seq_len // tl
    assert seq_len % tl == 0
    bias_prev, bias_cur, bias_next = _band_bias(tl)
    nt_dims = (((1,), (1,)), ((), ()))
    tn_dims = (((0,), (0,)), ((), ()))

    def body(q_ref, kp_ref, kc_ref, kn_ref, vp_ref, vc_ref, vn_ref, bp_ref, bc_ref, bn_ref,
             o_ref, lse_ref):
        pos = pl.program_id(0) % nb_seq
        b_prev = bp_ref[...] + jnp.where(pos == 0, MASK_BIAS, 0.0)
        b_next = bn_ref[...] + jnp.where(pos == nb_seq - 1, MASK_BIAS, 0.0)
        b_cur = bc_ref[...]
        lse_rows = []
        for h in range(C_HEADS):
            sl = slice(h * dh, (h + 1) * dh)
            q = q_ref[:, sl]
            s_p = lax.dot_general(kp_ref[:, sl], q, nt_dims, preferred_element_type=F32) + b_prev
            s_c = lax.dot_general(kc_ref[:, sl], q, nt_dims, preferred_element_type=F32) + b_cur
            s_n = lax.dot_general(kn_ref[:, sl], q, nt_dims, preferred_element_type=F32) + b_next
            m = jnp.maximum(jnp.max(s_c, axis=0, keepdims=True),
                            jnp.maximum(jnp.max(s_p, axis=0, keepdims=True),
                                        jnp.max(s_n, axis=0, keepdims=True)))
            p_p = jnp.exp2(s_p - m)
            p_c = jnp.exp2(s_c - m)
            p_n = jnp.exp2(s_n - m)
            l = (jnp.sum(p_c, axis=0, keepdims=True) + jnp.sum(p_p, axis=0, keepdims=True)
                 + jnp.sum(p_n, axis=0, keepdims=True))
            ot = (lax.dot_general(vc_ref[:, sl], p_c.astype(BF16), tn_dims, preferred_element_type=F32)
                  + lax.dot_general(vp_ref[:, sl], p_p.astype(BF16), tn_dims, preferred_element_type=F32)
                  + lax.dot_general(vn_ref[:, sl], p_n.astype(BF16), tn_dims, preferred_element_type=F32))
            o_ref[:, sl] = (ot / l).T
            lse_rows.append((m + jnp.log2(l)) * (1.0 / LOG2E))
        lse_t = jnp.concatenate(lse_rows + [jnp.zeros((LANES - C_HEADS, tl), F32)], axis=0)
        lse_ref[...] = lse_t.T

    def prev_halo(blk):
        return jnp.maximum(blk * halo_per_block - 1, 0)

    def next_halo(blk):
        return jnp.minimum((blk + 1) * halo_per_block, n_halo_blocks - 1)

    const = lambda blk: (0, 0)
    return pl.pallas_call(
        body,
        grid=(n_rows // tl,),
        in_specs=[
            pl.BlockSpec((tl, hw), lambda blk: (blk, 0)),
            pl.BlockSpec((C_REACH, hw), lambda blk: (prev_halo(blk), 1)),
            pl.BlockSpec((tl, hw), lambda blk: (blk, 1)),
            pl.BlockSpec((C_REACH, hw), lambda blk: (next_halo(blk), 1)),
            pl.BlockSpec((C_REACH, hw), lambda blk: (prev_halo(blk), 2)),
            pl.BlockSpec((tl, hw), lambda blk: (blk, 2)),
            pl.BlockSpec((C_REACH, hw), lambda blk: (next_halo(blk), 2)),
            pl.BlockSpec((C_REACH, tl), const),
            pl.BlockSpec((tl, tl), const),
            pl.BlockSpec((C_REACH, tl), const),
        ],
        out_specs=[pl.BlockSpec((tl, hw), lambda blk: (blk, 0)),
                   pl.BlockSpec((tl, LANES), lambda blk: (blk, 0))],
        out_shape=[jax.ShapeDtypeStruct((n_rows, hw), F32),
                   jax.ShapeDtypeStruct((n_rows, LANES), F32)],
        compiler_params=_params(1),
    )(x, x, x, x, x, x, x, bias_prev, bias_cur, bias_next)


def _merge_out_proj_residual(os, lses, w_o, x, *, batch, seq, tm=512, tn=2048):
    t_rows = x.shape[0]
    hw, n_dim = w_o.shape
    dh = C_HEAD_DIM
    dils = [dil for _, dil in C_PAIRS]
    blocks_per_seq = seq // tm
    assert all(tm % dil == 0 and (tm // dil) % 8 == 0 for dil in dils)

    def body(o0_ref, o1_ref, o2_ref, l0_ref, l1_ref, l2_ref, w_ref, res_ref, out_ref,
             ot_ref, lt_ref, mg_ref):
        @pl.when(pl.program_id(1) == 0)
        def _():
            for g, (o_ref, l_ref, dil) in enumerate(zip((o0_ref, o1_ref, o2_ref),
                                                        (l0_ref, l1_ref, l2_ref), dils)):
                for r in range(dil):
                    rows = pl.ds(r, tm // dil, stride=dil)
                    for h in range(C_HEADS):
                        ot_ref[g, h, rows, :] = o_ref[r, :, h * dh:(h + 1) * dh]
                    lt_ref[g, rows, :] = l_ref[r]
            lse = [lt_ref[g] for g in range(C_GROUPS)]
            mx = jnp.maximum(lse[0], jnp.maximum(lse[1], lse[2]))
            e = [jnp.exp(v - mx) for v in lse]
            den = e[0] + e[1] + e[2]
            wts = [v / den for v in e]
            for h in range(C_HEADS):
                sl = slice(h * dh, (h + 1) * dh)
                mg = (wts[0][:, h:h + 1] * ot_ref[0, h] + wts[1][:, h:h + 1] * ot_ref[1, h]
                      + wts[2][:, h:h + 1] * ot_ref[2, h])
                mg_ref[:, sl] = mg.astype(BF16)

        out_ref[...] = res_ref[...] + jnp.dot(mg_ref[...], w_ref[...], preferred_element_type=F32)

    def group_spec(dil, width):
        return pl.BlockSpec((dil, tm // dil, width),
                            lambda i, j: (i // blocks_per_seq, i % blocks_per_seq, 0))

    args = ([o.reshape(batch * dil, seq // dil, hw) for o, dil in zip(os, dils)]
            + [l.reshape(batch * dil, seq // dil, LANES) for l, dil in zip(lses, dils)])
    return pl.pallas_call(
        body,
        grid=(t_rows // tm, n_dim // tn),
        in_specs=([group_spec(dil, hw) for dil in dils] + [group_spec(dil, LANES) for dil in dils]
                  + [pl.BlockSpec((hw, tn), lambda i, j: (0, j)),
                     pl.BlockSpec((tm, tn), lambda i, j: (i, j))]),
        out_specs=pl.BlockSpec((tm, tn), lambda i, j: (i, j)),
        out_shape=jax.ShapeDtypeStruct((t_rows, n_dim), F32),
        scratch_shapes=[pltpu.VMEM((C_GROUPS, C_HEADS, tm, dh), F32),
                        pltpu.VMEM((C_GROUPS, tm, LANES), F32),
                        pltpu.VMEM((tm, hw), BF16)],
        compiler_params=_params(2),
    )(*args, w_o, x)


def _mlp_residual(x, gain, w_in, w_out, *, tm=1024, tf=512):
    t_rows, d = x.shape
    nf = w_in.shape[1] // tf

    def body(x_ref, g_ref, wi_ref, wo_ref, o_ref, xn_ref):
        @pl.when(pl.program_id(1) == 0)
        def _():
            _rms_to_bf16(x_ref, g_ref, xn_ref)
            o_ref[...] = x_ref[...]

        h = jnp.dot(xn_ref[...], wi_ref[...], preferred_element_type=F32)
        h = jnp.square(jnp.maximum(h, 0.0)).astype(BF16)
        o_ref[...] += jnp.dot(h, wo_ref[...], preferred_element_type=F32)

    return pl.pallas_call(
        body,
        grid=(t_rows // tm, nf),
        in_specs=[
            pl.BlockSpec((tm, d), lambda i, f: (i, 0)),
            pl.BlockSpec((1, d), lambda i, f: (0, 0)),
            pl.BlockSpec((d, tf), lambda i, f: (0, f)),
            pl.BlockSpec((tf, d), lambda i, f: (f, 0)),
        ],
        out_specs=pl.BlockSpec((tm, d), lambda i, f: (i, 0)),
        out_shape=jax.ShapeDtypeStruct((t_rows, d), F32),
        scratch_shapes=[pltpu.VMEM((tm, d), BF16)],
        compiler_params=_params(2),
    )(x, gain, w_in, w_out)


def _ple_residual(x, p, gate_g, w_gate, w_ple, ple_g, *, tm=512):
    t_rows, d = x.shape
    pd = p.shape[1]

    def body(x_ref, p_ref, gg_ref, wg_ref, wp_ref, pg_ref, o_ref, xn_ref):
        _rms_to_bf16(x_ref, gg_ref, xn_ref)
        e = jnp.dot(p_ref[...].astype(BF16), wp_ref[...], preferred_element_type=F32)
        ms = jnp.mean(e * e, axis=-1, keepdims=True)
        e = e * lax.rsqrt(ms + EPS) * pg_ref[...]
        z = jnp.dot(xn_ref[...], wg_ref[...], preferred_element_type=F32)
        o_ref[...] = x_ref[...] + jax.nn.sigmoid(z) * e

    const = lambda i: (0, 0)
    return pl.pallas_call(
        body,
        grid=(t_rows // tm,),
        in_specs=[
            pl.BlockSpec((tm, d), lambda i: (i, 0)),
            pl.BlockSpec((tm, pd), lambda i: (i, 0)),
            pl.BlockSpec((1, d), const),
            pl.BlockSpec((d, d), const, pipeline_mode=pl.Buffered(1)),
            pl.BlockSpec((pd, d), const, pipeline_mode=pl.Buffered(1)),
            pl.BlockSpec((1, d), const),
        ],
        out_specs=pl.BlockSpec((tm, d), lambda i: (i, 0)),
        out_shape=jax.ShapeDtypeStruct((t_rows, d), F32),
        scratch_shapes=[pltpu.VMEM((tm, d), BF16)],
        compiler_params=_params(1),
    )(x, p, gate_g, w_gate, w_ple, ple_g)


def _rope_cos_sin(pos, dim, theta):
    inv_freq = theta ** (-jnp.arange(0, dim, 2, dtype=F32) / dim)
    ang = pos.astype(F32)[:, None] * inv_freq[None, :]
    return jnp.cos(ang), jnp.sin(ang)


def _pair_perm(first, second, rest):
    fill = LANES // 2 - len(first)
    perm = list(first) + list(rest[:fill]) + list(second) + list(rest[fill:])
    assert sorted(perm) == list(range(LANES))
    return np.asarray(perm)


A_PERM = _pair_perm(list(range(0, 32)) + list(range(64, 96)),
                    list(range(32, 64)) + list(range(96, 128)), [])
B_PERM = _pair_perm(range(0, 32), range(32, 64), list(range(64, 128)))
C_PERM = _pair_perm(range(0, 16), range(16, 32), list(range(32, 128)))


def _permute_heads(w_cols, n_heads, perm):
    lead = w_cols.shape[:-1]
    n_cols = n_heads * LANES
    heads = w_cols[..., :n_cols].reshape(lead + (n_heads, LANES))[..., perm]
    return jnp.concatenate([heads.reshape(lead + (n_cols,)), w_cols[..., n_cols:]], axis=-1)


def _axial_tables(seq):
    t = jnp.arange(seq)
    half = A_HEAD_DIM // 2
    cr, sr = _rope_cos_sin(t // GRID_W, half, A_ROPE_THETA)
    cc, sc = _rope_cos_sin(t % GRID_W, half, A_ROPE_THETA)
    return (jnp.concatenate([cr, cc, cr, cc], axis=-1),
            jnp.concatenate([-sr, -sc, sr, sc], axis=-1))


def _partial_tables(seq, rope_dim, theta):
    c, s = _rope_cos_sin(jnp.arange(seq), rope_dim, theta)
    fill = LANES // 2 - rope_dim // 2
    one, zero = jnp.ones((seq, fill), F32), jnp.zeros((seq, fill), F32)
    return (jnp.concatenate([c, one, c, one], axis=-1),
            jnp.concatenate([-s, zero, s, zero], axis=-1))


def _row(v):
    return v.reshape(1, -1).astype(F32)


def _prep_weights(w):
    out = {}

    def per_layer(stack):
        return [stack[i].astype(BF16) for i in range(stack.shape[0])]

    for name in ("w_mlp_in", "w_mlp_out", "w_ple", "w_ple_gate", "a_w_o", "b_w_o", "c_w_o"):
        out[name] = per_layer(w[name])
    out["a_w_qkv"] = _permute_heads(w["a_w_qkv"], A_HEADS + A_KV_HEADS, A_PERM).astype(BF16)
    out["a_head_g"] = jnp.concatenate(
        [jnp.tile(w["a_q_norm_g"][:, A_PERM] * (A_HEAD_DIM ** -0.5 * LOG2E), (1, A_HEADS)),
         jnp.tile(w["a_k_norm_g"][:, A_PERM], (1, A_KV_HEADS)),
         jnp.ones((w["a_q_norm_g"].shape[0], A_KV_HEADS * A_HEAD_DIM), F32)], axis=-1)

    n_b = w["b_w_dqkv"].shape[0]
    lat_w = B_Q_RANK + B_KV_RANK + B_ROPE_DIM
    n_lat_groups = B_LAT_PAD // LANES

    def permute_last_group(a, n_groups):
        lead = a.shape[:-1]
        grouped = a.reshape(lead + (n_groups, LANES))
        last = grouped[..., n_groups - 1:, :][..., B_PERM]
        return jnp.concatenate([grouped[..., :n_groups - 1, :], last], axis=-2).reshape(a.shape)

    dqkv = jnp.pad(w["b_w_dqkv"], ((0, 0), (0, 0), (0, B_LAT_PAD - lat_w)))
    out["b_w_dqkv"] = permute_last_group(dqkv, n_lat_groups).astype(BF16)
    uq = w["b_w_uq"].reshape(n_b, B_Q_RANK, B_HEADS, B_QK_DIM)
    uq = jnp.pad(uq, ((0, 0), (0, 0), (0, 0), (0, B_HEAD_PAD - B_QK_DIM)))
    uq = permute_last_group(uq, B_HEAD_PAD // LANES)
    out["b_w_uq"] = uq.reshape(n_b, B_Q_RANK, B_HEADS * B_HEAD_PAD).astype(BF16)
    ukv = w["b_w_ukv"].reshape(n_b, B_KV_RANK, B_HEADS, B_NOPE_DIM + B_V_DIM)
    out["b_w_uk"] = ukv[..., :B_NOPE_DIM].reshape(n_b, B_KV_RANK, B_HEADS * B_NOPE_DIM).astype(BF16)
    out["b_w_uv"] = ukv[..., B_NOPE_DIM:].reshape(n_b, B_KV_RANK, B_HEADS * B_V_DIM).astype(BF16)
    pad_g = ((0, 0), (0, B_HEAD_PAD - B_QK_DIM))
    out["b_q_g"] = permute_last_group(
        jnp.pad(w["b_q_norm_g"] * (B_QK_DIM ** -0.5 * LOG2E), pad_g), B_HEAD_PAD // LANES)
    out["b_k_g"] = permute_last_group(jnp.pad(w["b_k_norm_g"], pad_g), B_HEAD_PAD // LANES)

    n_ch = C_GROUPS * C_HEADS
    hw = C_HEADS * C_HEAD_DIM

    def group_major(a):
        lead = a.shape[:-1]
        a = a.reshape(lead + (3, C_GROUPS, hw))
        return jnp.swapaxes(a, -3, -2).reshape(lead + (3 * C_GROUPS * hw,))

    out["c_w_qkv"] = group_major(_permute_heads(w["c_w_qkv"], 2 * n_ch, C_PERM)).astype(BF16)
    out["c_head_g"] = group_major(jnp.concatenate(
        [jnp.tile(w["c_q_norm_g"][:, C_PERM] * (C_HEAD_DIM ** -0.5 * LOG2E), (1, n_ch)),
         jnp.tile(w["c_k_norm_g"][:, C_PERM], (1, n_ch)),
         jnp.ones((w["c_q_norm_g"].shape[0], n_ch * C_HEAD_DIM), F32)], axis=-1))
    return out


def _mixer_a(x, norm_g, w, pw, j, batch, seq):
    cos, sin = _axial_tables(seq)
    w_qkv = pw["a_w_qkv"][j]
    qk, vt = _qkv_proj(x, w_qkv, norm_g, _row(pw["a_head_g"][j]), cos, sin, seq=seq,
                       n_normed_cols=(A_HEADS + A_KV_HEADS) * A_HEAD_DIM, tm=512,
                       tn=w_qkv.shape[1], n_transposed_cols=A_KV_HEADS * A_HEAD_DIM)
    group = A_HEADS // A_KV_HEADS
    o = _dense_attention(qk, qk, vt, batch=batch, seq=seq, n_groups=A_KV_HEADS,
                         heads=group, kv_heads=1, dq=A_HEAD_DIM,
                         q_cb0=0, k_cb0=A_HEADS)
    return _out_proj_residual(o, pw["a_w_o"][j], x)


def _mixer_b(x, norm_g, w, pw, j, batch, seq):
    cos, sin = _partial_tables(seq, B_ROPE_DIM, B_ROPE_THETA)
    tm = 512
    n_pos_blocks = seq // tm
    lat = _fused_matmul(x, pw["b_w_dqkv"][j], tm=tm, tn=B_LAT_PAD, chunk=B_LAT_PAD // 3,
                        gain=norm_g, out_dtype=F32)

    def pos_spec():
        return pl.BlockSpec((tm, LANES), lambda i, jj: (i % n_pos_blocks, 0))

    head_g_spec = pl.BlockSpec((1, B_HEAD_PAD), lambda i, jj: (0, 0))
    q = _fused_matmul(lat, pw["b_w_uq"][j], tm=tm, tn=B_HEADS * B_HEAD_PAD, chunk=512,
                      gain=_row(w["b_cq_norm_g"][j]), x_kblock=0,
                      extras=[(_row(pw["b_q_g"][j]), head_g_spec), (cos, pos_spec()), (sin, pos_spec())],
                      epilogue=_mla_q_epilogue)
    ckv_g = _row(w["b_ckv_norm_g"][j])
    k_rope_block = (B_Q_RANK + B_KV_RANK) // LANES
    k = _fused_matmul(lat, pw["b_w_uk"][j], tm=tm, tn=B_HEADS * B_NOPE_DIM, chunk=512,
                      gain=ckv_g, x_kblock=1,
                      extras=[(lat, pl.BlockSpec((tm, LANES), lambda i, jj: (i, k_rope_block))),
                              (_row(pw["b_k_g"][j]), head_g_spec), (cos, pos_spec()), (sin, pos_spec())],
                      epilogue=_mla_k_epilogue, out_tile=B_HEADS * B_HEAD_PAD)
    vt = _fused_matmul(lat, pw["b_w_uv"][j], tm=tm, tn=B_HEADS * B_V_DIM, chunk=512,
                       gain=ckv_g, x_kblock=1, n_transposed_cols=B_HEADS * B_V_DIM)
    heads = 4
    o = _dense_attention(q, k, vt, batch=batch, seq=seq, n_groups=B_HEADS // heads,
                         heads=heads, kv_heads=heads, dq=B_HEAD_PAD,
                         q_cb0=0, k_cb0=0)
    return _out_proj_residual(o, pw["b_w_o"][j], x)


def _mixer_c(x, norm_g, w, pw, j, batch, seq):
    cos, sin = _partial_tables(seq, C_ROPE_DIM, C_ROPE_THETA)
    n_ch = C_GROUPS * C_HEADS
    by_group = _dilated_qkv_proj(x, pw["c_w_qkv"][j], norm_g, _row(pw["c_head_g"][j]), cos, sin,
                                 batch=batch, seq=seq)
    os, lses = [], []
    for xg, (_, dil) in zip(by_group, C_PAIRS):
        o, lse = _banded_attention(xg.reshape(batch * seq, xg.shape[-1]), seq_len=seq // dil,
                                   tl=min(C_TL, seq // dil))
        os.append(o)
        lses.append(lse)
    return _merge_out_proj_residual(os, lses, pw["c_w_o"][j], x, batch=batch, seq=seq)


def _layer_stack(x, p, w, pw):
    batch, seq, d = x.shape
    x = x.reshape(batch * seq, d)
    p = p.reshape(DEPTH, batch * seq, PLE_DIM)
    for i in range(DEPTH):
        kind, j = i % N_MIXERS, i // N_MIXERS
        mixer = (_mixer_a, _mixer_b, _mixer_c)[kind]
        x = mixer(x, _row(w["norm_mix_g"][i]), w, pw, j, batch, seq)
        x = _mlp_residual(x, _row(w["norm_mlp_g"][i]), pw["w_mlp_in"][i], pw["w_mlp_out"][i])
        x = _ple_residual(x, p[i], _row(w["ple_gate_norm_g"][i]), pw["w_ple_gate"][i],
                          pw["w_ple"][i], _row(w["ple_norm_g"][i]))
    return x.reshape(batch, seq, d)


def kernel(x_prompt, x_sample, p_prompt, p_sample, norm_mix_g, norm_mlp_g, w_mlp_in, w_mlp_out, w_ple, ple_norm_g, ple_gate_norm_g, w_ple_gate, a_w_qkv, a_q_norm_g, a_k_norm_g, a_w_o, b_w_dqkv, b_cq_norm_g, b_ckv_norm_g, b_w_uq, b_w_ukv, b_q_norm_g, b_k_norm_g, b_w_o, c_w_qkv, c_q_norm_g, c_k_norm_g, c_w_o):
    w = dict(norm_mix_g=norm_mix_g, norm_mlp_g=norm_mlp_g, w_mlp_in=w_mlp_in, w_mlp_out=w_mlp_out,
             w_ple=w_ple, ple_norm_g=ple_norm_g, ple_gate_norm_g=ple_gate_norm_g,
             w_ple_gate=w_ple_gate, a_w_qkv=a_w_qkv, a_q_norm_g=a_q_norm_g, a_k_norm_g=a_k_norm_g,
             a_w_o=a_w_o, b_w_dqkv=b_w_dqkv, b_cq_norm_g=b_cq_norm_g, b_ckv_norm_g=b_ckv_norm_g,
             b_w_uq=b_w_uq, b_w_ukv=b_w_ukv, b_q_norm_g=b_q_norm_g, b_k_norm_g=b_k_norm_g,
             b_w_o=b_w_o, c_w_qkv=c_w_qkv, c_q_norm_g=c_q_norm_g, c_k_norm_g=c_k_norm_g, c_w_o=c_w_o)
    pw = _prep_weights(w)
    return (_layer_stack(x_prompt, p_prompt, w, pw), _layer_stack(x_sample, p_sample, w, pw))
```
